```python
import math
import jax, jax.numpy as jnp
from jax import lax
import numpy as np

D_MODEL = 1024
BATCH = 8
SEQ = 4096
DEPTH = 2

MIX_W = D_MODEL
SSD_HEAD_DIM = 64
SSD_INNER = MIX_W // 2
SSD_HEADS = SSD_INNER // SSD_HEAD_DIM
SSD_GROUPS = 2
SSD_HPG = SSD_HEADS // SSD_GROUPS
SSD_STATE = 128
SSD_CONV_K = 4
SSD_CHUNK = 256
SSD_CONV_CH = SSD_INNER + 2 * SSD_GROUPS * SSD_STATE
SSD_PROJ_W = SSD_INNER + SSD_CONV_CH + SSD_HEADS
POOL_W = MIX_W // 4
POOL_WINDOWS = (2, 4, 8, 16)
POOL_GROUPS = len(POOL_WINDOWS)
POOL_CH = POOL_W // POOL_GROUPS
ATT_HEAD_DIM = 64
ATT_W = MIX_W - SSD_INNER - POOL_W
ATT_HEADS = ATT_W // ATT_HEAD_DIM
ATT_PATTERNS = ((128, 1), (512, 4), (2048, 16))
ATT_BLOCK = 128
ROT_DIM = ATT_HEAD_DIM // 4
ROPE_THETA = 500000.0
IN_W = SSD_PROJ_W + POOL_W + 3 * ATT_W
FFN_DIM = 2816
FFN_CONV_K = 3
NORM_EPS = 1e-6

kernel_name = "hybrid_ssd_pool_dilated_attn_trunk"

F32 = jnp.float32


def rmsnorm(x, g):
    xf = x.astype(F32)
    y = xf * lax.rsqrt(jnp.mean(xf * xf, axis=-1, keepdims=True) + NORM_EPS)
    return (y * g.astype(F32)).astype(x.dtype)


def causal_dwconv(x, w, b):
    k, ch = w.shape
    y = lax.conv_general_dilated(x, w.astype(x.dtype)[:, None, :], window_strides=(1,),
                                 padding=[(k - 1, 0)], dimension_numbers=("NWC", "WIO", "NWC"),
                                 feature_group_count=ch)
    return y + b.astype(x.dtype)


def ssd_chunked(xs, da, bm, cm):
    b, s, g, j, p = xs.shape
    n = bm.shape[-1]
    pad = (-s) % SSD_CHUNK
    sp = s + pad
    nc = sp // SSD_CHUNK
    q = SSD_CHUNK
    xs = jnp.pad(xs, ((0, 0), (0, pad), (0, 0), (0, 0), (0, 0))).reshape(b, nc, q, g, j, p)
    da = jnp.pad(da, ((0, 0), (0, pad), (0, 0), (0, 0))).reshape(b, nc, q, g, j)
    bm = jnp.pad(bm, ((0, 0), (0, pad), (0, 0), (0, 0))).reshape(b, nc, q, g, n)
    cm = jnp.pad(cm, ((0, 0), (0, pad), (0, 0), (0, 0))).reshape(b, nc, q, g, n)
    a_cum = jnp.cumsum(da, axis=2)
    acs = jnp.moveaxis(a_cum, 2, -1)
    seg = acs[..., :, None] - acs[..., None, :]
    causal = jnp.tril(jnp.ones((q, q), dtype=bool))
    lmat = jnp.exp(jnp.where(causal, seg, -jnp.inf))
    cb = jnp.einsum("bclgn,bcsgn->bcgls", cm, bm)
    y_diag = jnp.einsum("bcgjls,bcsgjp->bclgjp", cb[:, :, :, None] * lmat, xs)
    decay_states = jnp.exp(a_cum[:, :, -1:] - a_cum)
    states = jnp.einsum("bclgn,bclgj,bclgjp->bcgjpn", bm, decay_states, xs)
    chunk_decay = jnp.exp(a_cum[:, :, -1])

    def step(h, inp):
        st, dec = inp
        return h * dec[..., None, None] + st, h

    h0 = jnp.zeros((b, g, j, p, n), xs.dtype)
    _, h_in = lax.scan(step, h0, (jnp.moveaxis(states, 1, 0), jnp.moveaxis(chunk_decay, 1, 0)))
    h_in = jnp.moveaxis(h_in, 0, 1)
    y_off = jnp.einsum("bclgn,bcgjpn,bclgj->bclgjp", cm, h_in, jnp.exp(a_cum))
    return (y_diag + y_off).reshape(b, sp, g, j, p)[:, :s]


def ssd_mixer(p_in, conv_w, conv_b, dt_bias, a_log, d_skip, norm_g):
    b, s, _ = p_in.shape
    z = p_in[..., :SSD_INNER].astype(F32)
    xbc = p_in[..., SSD_INNER:SSD_INNER + SSD_CONV_CH]
    dt = p_in[..., SSD_INNER + SSD_CONV_CH:].astype(F32)
    xbc = jax.nn.silu(causal_dwconv(xbc, conv_w, conv_b)).astype(F32)
    xs = xbc[..., :SSD_INNER].reshape(b, s, SSD_GROUPS, SSD_HPG, SSD_HEAD_DIM)
    bm = xbc[..., SSD_INNER:SSD_INNER + SSD_GROUPS * SSD_STATE].reshape(b, s, SSD_GROUPS, SSD_STATE)
    cm = xbc[..., SSD_INNER + SSD_GROUPS * SSD_STATE:].reshape(b, s, SSD_GROUPS, SSD_STATE)
    dt = jax.nn.softplus(dt + dt_bias.astype(F32)).reshape(b, s, SSD_GROUPS, SSD_HPG)
    a = -jnp.exp(a_log.astype(F32)).reshape(SSD_GROUPS, SSD_HPG)
    y = ssd_chunked(xs * dt[..., None], dt * a, bm, cm)
    y = y + d_skip.astype(F32).reshape(SSD_GROUPS, SSD_HPG, 1) * xs
    y = y.reshape(b, s, SSD_GROUPS, SSD_HPG * SSD_HEAD_DIM) * jax.nn.silu(z).reshape(b, s, SSD_GROUPS, -1)
    y = y * lax.rsqrt(jnp.mean(y * y, axis=-1, keepdims=True) + NORM_EPS)
    y = y * norm_g.astype(F32).reshape(SSD_GROUPS, -1)
    return y.reshape(b, s, SSD_INNER)


def pool_mixer(u, pool_w, pool_scale):
    b, s, _ = u.shape
    u = u.astype(F32).reshape(b, s, POOL_GROUPS, POOL_CH)
    cs = jnp.cumsum(u, axis=1)
    outs = []
    for gi, w in enumerate(POOL_WINDOWS):
        cg = cs[:, :, gi]
        lag = jnp.pad(cg, ((0, 0), (w, 0), (0, 0)))[:, :s]
        cnt = jnp.minimum(jnp.arange(1, s + 1), w).astype(F32)[None, :, None]
        outs.append((cg - lag) / cnt)
    pooled = jnp.stack(outs, axis=2)
    y = jnp.einsum("bsgc,gcd->bsgd", pooled - u, pool_w.astype(F32))
    return y.reshape(b, s, POOL_W) * pool_scale.astype(F32)


def partial_rope(t, cos, sin):
    half = ROT_DIM // 2
    t1, t2 = t[..., :half], t[..., half:ROT_DIM]
    c, s = cos[:, :, None], sin[:, :, None]
    return jnp.concatenate([t1 * c - t2 * s, t2 * c + t1 * s, t[..., ROT_DIM:]], axis=-1)


def dilated_branch(q, k, v, window, dilation):
    b, s, h, e = q.shape
    L = s // dilation
    steps = window // dilation
    n_prev = -(-steps // ATT_BLOCK)
    nb = -(-L // ATT_BLOCK)
    lp = nb * ATT_BLOCK

    def strided(t):
        t = t.reshape(b, L, dilation, h, e).transpose(0, 2, 3, 1, 4)
        t = jnp.pad(t, ((0, 0), (0, 0), (0, 0), (0, lp - L), (0, 0)))
        return t.reshape(b, dilation, h, nb, ATT_BLOCK, e)

    def band(t):
        tp = jnp.pad(t, ((0, 0), (0, 0), (0, 0), (n_prev, 0), (0, 0), (0, 0)))
        return jnp.concatenate([tp[:, :, :, j:j + nb] for j in range(n_prev + 1)], axis=4)

    qb, kb, vb = strided(q), strided(k), strided(v)
    kband, vband = band(kb), band(vb)
    sc = jnp.einsum("bdhnqe,bdhnke->bdhnqk", qb, kband) * (e ** -0.5)
    qi = jnp.arange(ATT_BLOCK)[:, None] + n_prev * ATT_BLOCK
    kj = jnp.arange((n_prev + 1) * ATT_BLOCK)[None, :]
    rel = qi - kj
    kpos = jnp.arange(nb)[:, None, None] * ATT_BLOCK + kj[None] - n_prev * ATT_BLOCK
    valid = (rel >= 0) & (rel <= steps) & (kpos >= 0)
    sc = jnp.where(valid, sc, -jnp.inf)
    m = jnp.max(sc, axis=-1, keepdims=True)
    p = jnp.exp(sc - m)
    l = jnp.sum(p, axis=-1)
    o = jnp.einsum("bdhnqk,bdhnke->bdhnqe", p, vband) / l[..., None]
    lse = m[..., 0] + jnp.log(l)
    o = o.reshape(b, dilation, h, lp, e)[:, :, :, :L].transpose(0, 3, 1, 2, 4).reshape(b, s, h, e)
    lse = lse.reshape(b, dilation, h, lp)[..., :L].transpose(0, 3, 1, 2).reshape(b, s, h)
    return o, lse


def dilated_attention(qkv, cos, sin):
    b, s, _ = qkv.shape
    qkv = qkv.astype(F32)
    q = partial_rope(qkv[..., :ATT_W].reshape(b, s, ATT_HEADS, ATT_HEAD_DIM), cos, sin)
    k = partial_rope(qkv[..., ATT_W:2 * ATT_W].reshape(b, s, ATT_HEADS, ATT_HEAD_DIM), cos, sin)
    v = qkv[..., 2 * ATT_W:].reshape(b, s, ATT_HEADS, ATT_HEAD_DIM)
    outs, lses = [], []
    for window, dilation in ATT_PATTERNS:
        o, lse = dilated_branch(q, k, v, window, dilation)
        outs.append(o)
        lses.append(lse)
    wts = jax.nn.softmax(jnp.stack(lses, axis=0), axis=0)
    o = jnp.einsum("rbsh,rbshe->bshe", wts, jnp.stack(outs, axis=0))
    return o.reshape(b, s, ATT_W)


def conv_ffn(h, up, conv_w, conv_b, down):
    hid = causal_dwconv(h @ up, conv_w, conv_b)
    g, u = jnp.split(hid, 2, axis=-1)
    return (jax.nn.silu(g) * u) @ down


def setup_inputs(seed: int = 0) -> dict:
    key = jax.random.key(seed)
    ks = jax.random.split(key, 24)
    nrm = lambda k, shape, scale: jax.random.normal(k, shape, F32) * scale
    dt = jnp.exp(jax.random.uniform(ks[9], (DEPTH, SSD_HEADS), F32) * (math.log(0.1) - math.log(0.001))
                 + math.log(0.001))
    return {
        "x": nrm(ks[0], (BATCH, SEQ, D_MODEL), 1.0),
        "c": nrm(ks[1], (BATCH, D_MODEL), 1.0),
        "positions": (jax.random.randint(ks[2], (BATCH, 1), 0, 1024, jnp.int32)
                      + jnp.arange(SEQ, dtype=jnp.int32)[None, :]),
        "ada_w": nrm(ks[3], (DEPTH, D_MODEL, 6 * D_MODEL), 0.5 * D_MODEL ** -0.5),
        "ada_b": nrm(ks[4], (DEPTH, 6 * D_MODEL), 0.02),
        "norm1_g": 1.0 + nrm(ks[5], (DEPTH, D_MODEL), 0.02),
        "w_in": nrm(ks[6], (DEPTH, D_MODEL, IN_W), D_MODEL ** -0.5),
        "ssd_conv_w": nrm(ks[7], (DEPTH, SSD_CONV_K, SSD_CONV_CH), SSD_CONV_K ** -0.5),
        "ssd_conv_b": nrm(ks[8], (DEPTH, SSD_CONV_CH), 0.02),
        "ssd_dt_bias": dt + jnp.log(-jnp.expm1(-dt)),
        "ssd_a_log": jnp.log(jax.random.uniform(ks[10], (DEPTH, SSD_HEADS), F32, 1.0, 16.0)),
        "ssd_d": 1.0 + nrm(ks[11], (DEPTH, SSD_HEADS), 0.02),
        "ssd_norm_g": 1.0 + nrm(ks[12], (DEPTH, SSD_INNER), 0.02),
        "pool_w": nrm(ks[13], (DEPTH, POOL_GROUPS, POOL_CH, POOL_CH), POOL_CH ** -0.5),
        "pool_scale": 1.0 + nrm(ks[14], (DEPTH, POOL_W), 0.02),
        "w_out": nrm(ks[15], (DEPTH, MIX_W, D_MODEL), MIX_W ** -0.5),
        "norm2_g": 1.0 + nrm(ks[16], (DEPTH, D_MODEL), 0.02),
        "ffn_up": nrm(ks[17], (DEPTH, D_MODEL, 2 * FFN_DIM), D_MODEL ** -0.5),
        "ffn_conv_w": nrm(ks[18], (DEPTH, FFN_CONV_K, 2 * FFN_DIM), FFN_CONV_K ** -0.5),
        "ffn_conv_b": nrm(ks[19], (DEPTH, 2 * FFN_DIM), 0.02),
        "ffn_down": nrm(ks[20], (DEPTH, FFN_DIM, D_MODEL), FFN_DIM ** -0.5),
        "final_g": 1.0 + nrm(ks[21], (D_MODEL,), 0.02),
    }


def reference(x, c, positions, ada_w, ada_b, norm1_g, w_in, ssd_conv_w, ssd_conv_b, ssd_dt_bias,
              ssd_a_log, ssd_d, ssd_norm_g, pool_w, pool_scale, w_out, norm2_g, ffn_up, ffn_conv_w,
              ffn_conv_b, ffn_down, final_g):
    inv_freq = ROPE_THETA ** (-jnp.arange(0, ROT_DIM, 2, dtype=F32) / ROT_DIM)
    ang = positions.astype(F32)[..., None] * inv_freq
    cos, sin = jnp.cos(ang), jnp.sin(ang)
    c_act = jax.nn.silu(c)
    a0 = SSD_PROJ_W
    a1 = SSD_PROJ_W + POOL_W
    for i in range(DEPTH):
        mod = (c_act @ ada_w[i] + ada_b[i])[:, None, :]
        sh1, sc1, g1, sh2, sc2, g2 = jnp.split(mod, 6, axis=-1)
        h = rmsnorm(x, norm1_g[i]) * (1.0 + sc1) + sh1
        proj = h @ w_in[i]
        y_ssd = ssd_mixer(proj[..., :a0], ssd_conv_w[i], ssd_conv_b[i], ssd_dt_bias[i],
                          ssd_a_log[i], ssd_d[i], ssd_norm_g[i])
        y_pool = pool_mixer(proj[..., a0:a1], pool_w[i], pool_scale[i])
        y_att = dilated_attention(proj[..., a1:], cos, sin)
        mix = jnp.concatenate([y_ssd, y_pool, y_att], axis=-1).astype(x.dtype)
        x = x + g2.dtype.type(1) * g1 * (mix @ w_out[i]) if False else x + g1 * (mix @ w_out[i])
        h = rmsnorm(x, norm2_g[i]) * (1.0 + sc2) + sh2
        x = x + g2 * conv_ffn(h, ffn_up[i], ffn_conv_w[i], ffn_conv_b[i], ffn_down[i])
    return rmsnorm(x, final_g)
```

```python
import functools
import math

import jax
import jax.numpy as jnp
from jax import lax
from jax.experimental import pallas as pl
from jax.experimental.pallas import tpu as pltpu

F32 = jnp.float32
BF16 = jnp.bfloat16

LANES = 128
SUBLANES = 8
VMEM_LIMIT = 56 * 1024 * 1024

NORM_EPS = 1e-6
ROPE_THETA = 500000.0

SSD_HEAD_DIM = 64
SSD_GROUPS = 2
SSD_STATE = 128
SSD_CONV_K = 4
SSD_CHUNK = 256
POOL_WINDOWS = (2, 4, 8, 16)
ATT_HEAD_DIM = 64
ATT_PATTERNS = ((128, 1), (512, 4), (2048, 16))
ATT_BLOCK = 128
ROT_DIM = ATT_HEAD_DIM // 4
FFN_CONV_K = 3

CONV_HALO = SUBLANES
POOL_HALO = 2 * SUBLANES


def _params(*semantics):
    return pltpu.CompilerParams(dimension_semantics=semantics, vmem_limit_bytes=VMEM_LIMIT)


def _silu(v):
    return v * jax.nn.sigmoid(v)


def _dot(a, b):
    return jnp.dot(a, b, preferred_element_type=F32)


def _dot_nt(a, b):
    return lax.dot_general(a, b, (((1,), (1,)), ((), ())), preferred_element_type=F32)


def _dot_tn(a, b):
    return lax.dot_general(a, b, (((0,), (0,)), ((), ())), preferred_element_type=F32)


def _ada_kernel(c_ref, w_ref, b_ref, o_ref):
    c_act = _silu(c_ref[...])
    o_ref[0] = jnp.dot(c_act, w_ref[0], preferred_element_type=F32,
                       precision=lax.Precision.HIGHEST) + b_ref[0]


def _ada_mod(c, ada_w, ada_b):
    depth, d, n = ada_w.shape
    b = c.shape[0]
    tn = d
    return pl.pallas_call(
        _ada_kernel,
        out_shape=jax.ShapeDtypeStruct((depth, b, n), F32),
        grid=(depth, n // tn),
        in_specs=[
            pl.BlockSpec((b, d), lambda i, j: (0, 0)),
            pl.BlockSpec((1, d, tn), lambda i, j: (i, 0, j)),
            pl.BlockSpec((1, 1, tn), lambda i, j: (i, 0, j)),
        ],
        out_specs=pl.BlockSpec((1, b, tn), lambda i, j: (i, 0, j)),
        compiler_params=_params("arbitrary", "arbitrary"),
        name="ada_mod",
    )(c, ada_w, ada_b.reshape(depth, 1, n))


def _rope_kernel(pos_ref, freq_ref, sign_ref, cos_ref, sin_ref):
    ang = pos_ref[0] * freq_ref[...]
    cos_ref[0] = jnp.cos(ang)
    sin_ref[0] = jnp.sin(ang) * sign_ref[...]


def _rope_tables(positions):
    b, s = positions.shape
    half = ROT_DIM // 2
    inv_freq = ROPE_THETA ** (-jnp.arange(0, ROT_DIM, 2, dtype=F32) / ROT_DIM)
    lane = jnp.arange(LANES) % ATT_HEAD_DIM
    freq = jnp.where(lane < ROT_DIM, inv_freq[lane % half], 0.0).astype(F32)[None, :]
    sign = jnp.where(lane < half, -1.0, jnp.where(lane < ROT_DIM, 1.0, 0.0)).astype(F32)[None, :]
    tm = min(s, 1024)
    return pl.pallas_call(
        _rope_kernel,
        out_shape=(jax.ShapeDtypeStruct((b, s, LANES), F32),) * 2,
        grid=(b, s // tm),
        in_specs=[
            pl.BlockSpec((1, tm, 1), lambda i, j: (i, j, 0)),
            pl.BlockSpec((1, LANES), lambda i, j: (0, 0)),
            pl.BlockSpec((1, LANES), lambda i, j: (0, 0)),
        ],
        out_specs=(pl.BlockSpec((1, tm, LANES), lambda i, j: (i, j, 0)),) * 2,
        compiler_params=_params("arbitrary", "arbitrary"),
        name="rope_tables",
    )(positions.astype(F32)[..., None], freq, sign)


def _modulated_norm(x, gain, shift, scale):
    y = x * lax.rsqrt(jnp.mean(x * x, axis=-1, keepdims=True) + NORM_EPS)
    return (y * gain) * (1.0 + scale) + shift


def _rotate(t, cos, sin):
    half = ROT_DIM // 2
    lane = lax.broadcasted_iota(jnp.int32, t.shape, 1) % ATT_HEAD_DIM
    partner = jnp.where(lane < half, pltpu.roll(t, LANES - half, 1), pltpu.roll(t, half, 1))
    return t * cos + partner * sin


def _inproj_kernel(widths, x_ref, mod_ref, g_ref, w_ref, wdt_ref, cos_ref, sin_ref,
                   z_ref, xbc_ref, dt_ref, u_ref, q_ref, k_ref, v_ref):
    inner, conv_ch, pool_w, att_w = widths
    h = _modulated_norm(x_ref[0], g_ref[...], mod_ref[0, 0:1, :], mod_ref[0, 1:2, :]).astype(BF16)
    o = 0
    z_ref[0] = _dot(h, w_ref[:, o:o + inner])
    o += inner
    xbc_ref[0] = _dot(h, w_ref[:, o:o + conv_ch])
    o += conv_ch
    u_ref[0] = _dot(h, w_ref[:, o:o + pool_w])
    o += pool_w
    dt_ref[0] = _dot(h, wdt_ref[...])
    cos, sin = cos_ref[0], sin_ref[0]
    scale = ATT_HEAD_DIM ** -0.5
    for p in range(att_w // LANES):
        q = _dot(h, w_ref[:, o + p * LANES:o + (p + 1) * LANES])
        q_ref[p, 0] = (_rotate(q, cos, sin) * scale).astype(BF16)
        k = _dot(h, w_ref[:, o + att_w + p * LANES:o + att_w + (p + 1) * LANES])
        k_ref[p, 0] = _rotate(k, cos, sin).astype(BF16)
        v = _dot(h, w_ref[:, o + 2 * att_w + p * LANES:o + 2 * att_w + (p + 1) * LANES])
        v_ref[p, 0] = v.astype(BF16)


def _in_projection(x, mod, gain, w_main, w_dt, cos, sin, widths, tm):
    b, s, d = x.shape
    inner, conv_ch, pool_w, att_w = widths
    pairs = att_w // LANES
    row = lambda i, j: (i, j, 0)
    fixed2 = lambda i, j: (0, 0)
    pair_spec = pl.BlockSpec((pairs, 1, tm, LANES), lambda i, j: (0, i, j, 0))
    pair_shape = jax.ShapeDtypeStruct((pairs, b, s, LANES), BF16)
    return pl.pallas_call(
        functools.partial(_inproj_kernel, widths),
        out_shape=(
            jax.ShapeDtypeStruct((b, s, inner), F32),
            jax.ShapeDtypeStruct((b, s, conv_ch), F32),
            jax.ShapeDtypeStruct((b, s, LANES), F32),
            jax.ShapeDtypeStruct((b, s, pool_w), F32),
            pair_shape, pair_shape, pair_shape,
        ),
        grid=(b, s // tm),
        in_specs=[
            pl.BlockSpec((1, tm, d), row),
            pl.BlockSpec((1, SUBLANES, d), lambda i, j: (i, 0, 0)),
            pl.BlockSpec((1, d), fixed2),
            pl.BlockSpec(w_main.shape, fixed2),
            pl.BlockSpec(w_dt.shape, fixed2),
            pl.BlockSpec((1, tm, LANES), row),
            pl.BlockSpec((1, tm, LANES), row),
        ],
        out_specs=(
            pl.BlockSpec((1, tm, inner), row),
            pl.BlockSpec((1, tm, conv_ch), row),
            pl.BlockSpec((1, tm, LANES), row),
            pl.BlockSpec((1, tm, pool_w), row),
            pair_spec, pair_spec, pair_spec,
        ),
        compiler_params=_params("arbitrary", "arbitrary"),
        name="in_projection",
    )(x, mod, gain, w_main, w_dt, cos, sin)


def _pair_rep(col, h0):
    lane = lax.broadcasted_iota(jnp.int32, (col.shape[0], LANES), 1)
    return jnp.where(lane < SSD_HEAD_DIM, col[:, h0:h0 + 1], col[:, h0 + 1:h0 + 2])


def _ssd_pool_kernel(dims, z_ref, xbc_ref, dt_ref, u_ref, cw_ref, cb_ref, dtb_ref, alog_ref, dskip_ref,
                     ng_ref, pw_ref, ps_ref, o_ref, xbuf, ubuf, hstate):
    inner, heads = dims
    q = SSD_CHUNK
    gw = inner // SSD_GROUPS
    n = SSD_STATE
    chunk = pl.program_id(1)

    @pl.when(chunk == 0)
    def _():
        xbuf[0:CONV_HALO, :] = jnp.zeros((CONV_HALO, xbuf.shape[1]), F32)
        ubuf[0:POOL_HALO, :] = jnp.zeros((POOL_HALO, ubuf.shape[1]), F32)
        hstate[...] = jnp.zeros(hstate.shape, F32)

    xbuf[CONV_HALO:CONV_HALO + q, :] = xbc_ref[0]
    conv = cb_ref[...]
    for k in range(SSD_CONV_K):
        off = CONV_HALO - (SSD_CONV_K - 1) + k
        conv = conv + cw_ref[k:k + 1, :] * xbuf[off:off + q, :]
    xbc = _silu(conv)
    xbuf[0:CONV_HALO, :] = xbuf[q:q + CONV_HALO, :]

    lane = lax.broadcasted_iota(jnp.int32, (q, LANES), 1)
    dt = jax.nn.softplus(dt_ref[0] + dtb_ref[...])
    a = -jnp.exp(alog_ref[...])
    da = jnp.where(lane < heads, dt * a, 0.0)
    rows = lax.broadcasted_iota(jnp.int32, (q, q), 0)
    cols = lax.broadcasted_iota(jnp.int32, (q, q), 1)
    causal = rows >= cols
    tri = jnp.where(causal, 1.0, 0.0).astype(F32)
    a_col = jnp.dot(tri, da, preferred_element_type=F32, precision=lax.Precision.HIGHEST)
    a_row = a_col.T

    pair_lane = lax.broadcasted_iota(jnp.int32, (q, LANES), 1)
    left = pair_lane < SSD_HEAD_DIM
    hpg = heads // SSD_GROUPS
    y_groups = []
    for g in range(SSD_GROUPS):
        xs = xbc[:, g * gw:(g + 1) * gw]
        bm = xbc[:, inner + g * n:inner + (g + 1) * n].astype(BF16)
        cm = xbc[:, inner + SSD_GROUPS * n + g * n:inner + SSD_GROUPS * n + (g + 1) * n].astype(BF16)
        cb = _dot_nt(cm, bm)
        y_off = _dot(cm, hstate[g].astype(BF16))
        y_pairs, xd_pairs, cd_pairs = [], [], []
        for p in range(hpg // 2):
            h0 = g * hpg + 2 * p
            xs_p = xs[:, p * LANES:(p + 1) * LANES]
            acum = _pair_rep(a_col, h0)
            xdt = xs_p * _pair_rep(dt, h0)
            y_p = y_off[:, p * LANES:(p + 1) * LANES] * jnp.exp(acum)
            for side in range(2):
                hh = h0 + side
                seg = a_col[:, hh:hh + 1] - a_row[hh:hh + 1, :]
                lmat = jnp.exp(jnp.where(causal, seg, -jnp.inf))
                keep = left if side == 0 else jnp.logical_not(left)
                y_p = y_p + _dot((cb * lmat).astype(BF16), jnp.where(keep, xdt, 0.0).astype(BF16))
            a_last = acum[q - 1:q, :]
            xd_pairs.append((xdt * jnp.exp(a_last - acum)).astype(BF16))
            cd_pairs.append(jnp.exp(a_last))
            y_pairs.append(y_p + dskip_ref[:, h0 * SSD_HEAD_DIM:(h0 + 2) * SSD_HEAD_DIM] * xs_p)
        xd = jnp.concatenate(xd_pairs, axis=1)
        hstate[g] = hstate[g] * jnp.concatenate(cd_pairs, axis=1) + _dot_tn(bm, xd)
        y = jnp.concatenate(y_pairs, axis=1) * _silu(z_ref[0, :, g * gw:(g + 1) * gw])
        y = y * lax.rsqrt(jnp.mean(y * y, axis=-1, keepdims=True) + NORM_EPS)
        y_groups.append(y * ng_ref[:, g * gw:(g + 1) * gw])
    o_ref[0, :, 0:inner] = jnp.concatenate(y_groups, axis=1).astype(o_ref.dtype)

    pool_w = ubuf.shape[1]
    pool_ch = pool_w // len(POOL_WINDOWS)
    u = u_ref[0]
    ubuf[POOL_HALO:POOL_HALO + q, :] = u
    plane = lax.broadcasted_iota(jnp.int32, (q, pool_w), 1)
    pos = chunk * q + lax.broadcasted_iota(jnp.int32, (q, pool_w), 0)
    acc = u
    pooled = jnp.zeros((q, pool_w), F32)
    win = jnp.zeros((q, pool_w), jnp.int32)
    done = 1
    for gi, w in enumerate(POOL_WINDOWS):
        for k in range(done, w):
            acc = acc + ubuf[POOL_HALO - k:POOL_HALO - k + q, :]
        done = w
        in_group = (plane >= gi * pool_ch) & (plane < (gi + 1) * pool_ch)
        pooled = jnp.where(in_group, acc, pooled)
        win = jnp.where(in_group, w, win)
    cnt = jnp.minimum(pos + 1, win).astype(F32)
    diff = (pooled / cnt - u).astype(BF16)
    ubuf[0:POOL_HALO, :] = ubuf[q:q + POOL_HALO, :]
    o_ref[0, :, inner:inner + pool_w] = (_dot(diff, pw_ref[...]) * ps_ref[...]).astype(o_ref.dtype)


def _ssd_pool(z, xbc, dt, u, cw, cb, dtb, alog, dskip, ng, pw, ps, heads):
    b, s, inner = z.shape
    conv_ch = xbc.shape[-1]
    pool_w = u.shape[-1]
    q = SSD_CHUNK
    row = lambda i, j: (i, j, 0)
    fixed2 = lambda i, j: (0, 0)
    full = lambda a: pl.BlockSpec(a.shape, fixed2)
    out_w = inner + pool_w
    return pl.pallas_call(
        functools.partial(_ssd_pool_kernel, (inner, heads)),
        out_shape=jax.ShapeDtypeStruct((b, s, out_w), BF16),
        grid=(b, s // q),
        in_specs=[
            pl.BlockSpec((1, q, inner), row),
            pl.BlockSpec((1, q, conv_ch), row),
            pl.BlockSpec((1, q, LANES), row),
            pl.BlockSpec((1, q, pool_w), row),
            full(cw), full(cb), full(dtb), full(alog), full(dskip), full(ng), full(pw), full(ps),
        ],
        out_specs=pl.BlockSpec((1, q, out_w), row),
        scratch_shapes=[
            pltpu.VMEM((CONV_HALO + q, conv_ch), F32),
            pltpu.VMEM((POOL_HALO + q, pool_w), F32),
            pltpu.VMEM((SSD_GROUPS, SSD_STATE, inner // SSD_GROUPS), F32),
        ],
        compiler_params=_params("arbitrary", "arbitrary"),
        name="ssd_pool",
    )(z, xbc, dt, u, cw, cb, dtb, alog, dskip, ng, pw, ps)


def _attn_kernel(seq, *refs):
    nbr = len(ATT_PATTERNS)
    qkv_refs = refs[:3 * nbr]
    o_ref = refs[3 * nbr]
    acc_ref, m_ref, l_ref = refs[3 * nbr + 1:]
    blk = ATT_BLOCK
    hd = ATT_HEAD_DIM
    lane = lax.broadcasted_iota(jnp.int32, (blk, LANES), 1)
    left = lane < hd
    qi = lax.broadcasted_iota(jnp.int32, (blk, blk), 0)
    kj = lax.broadcasted_iota(jnp.int32, (blk, blk), 1)
    mask_cur = kj <= qi
    mask_prev = kj >= qi
    neg = -jnp.inf

    def block_step(first, q_ref, k_ref, v_ref, col, n, rows):
        r0 = pl.multiple_of(n * blk, blk)
        rp = pl.multiple_of(jnp.maximum(n - 1, 0) * blk, blk)
        qb = q_ref[0, 0, pl.ds(r0, blk), col:col + LANES]
        kc = k_ref[0, 0, pl.ds(r0, blk), col:col + LANES]
        kp = k_ref[0, 0, pl.ds(rp, blk), col:col + LANES]
        vc = v_ref[0, 0, pl.ds(r0, blk), col:col + LANES]
        vp = v_ref[0, 0, pl.ds(rp, blk), col:col + LANES]
        prev_pen = jnp.where(n > 0, 0.0, neg).astype(F32)
        if first:
            m_old = jnp.full((blk, LANES), neg, F32)
        else:
            m_old = m_ref[rows, :]
        zero = jnp.zeros_like(qb)
        one = jnp.ones_like(vc)
        m_new = m_old
        sides = []
        for side in range(2):
            keep = left if side == 0 else jnp.logical_not(left)
            qh = jnp.where(keep, qb, zero)
            s_c = jnp.where(mask_cur, _dot_nt(qh, kc), neg)
            s_p = jnp.where(mask_prev, _dot_nt(qh, kp) + prev_pen, neg)
            m_h = jnp.maximum(jnp.max(s_c, axis=-1, keepdims=True), jnp.max(s_p, axis=-1, keepdims=True))
            sides.append((s_c, s_p, m_h))
            m_new = jnp.where(keep, jnp.maximum(m_old, m_h), m_new)
        res = []
        for side in range(2):
            keep = left if side == 0 else jnp.logical_not(left)
            s_c, s_p, _ = sides[side]
            m_col = m_new[:, side * hd:side * hd + 1]
            p_c = jnp.exp(s_c - m_col).astype(BF16)
            p_p = jnp.exp(s_p - m_col).astype(BF16)
            res.append(_dot(p_c, jnp.where(keep, vc, one)) + _dot(p_p, jnp.where(keep, vp, one)))
        acc_blk = jnp.where(left, res[0], res[1])
        l_blk = jnp.where(left, res[1], res[0])
        if first:
            acc_new, l_new = acc_blk, l_blk
        else:
            alpha = jnp.exp(m_old - m_new)
            alpha_sw = pltpu.roll(alpha, hd, 1)
            acc_new = alpha * acc_ref[rows, :] + acc_blk
            l_new = alpha_sw * l_ref[rows, :] + l_blk
        return acc_new, m_new, l_new

    for bi, (window, dil) in enumerate(ATT_PATTERNS):
        assert window // dil == blk
        q_ref, k_ref, v_ref = qkv_refs[3 * bi:3 * bi + 3]
        nblk = seq // dil // blk
        first = bi == 0
        last = bi == nbr - 1
        for r in range(dil):
            col = r * LANES

            def body(n, carry, first=first, last=last, dil=dil, r=r, col=col,
                     q_ref=q_ref, k_ref=k_ref, v_ref=v_ref):
                if dil == 1:
                    rows = pl.ds(pl.multiple_of(n * blk, blk), blk)
                else:
                    rows = pl.ds(n * (blk * dil) + r, blk, stride=dil)
                acc_new, m_new, l_new = block_step(first, q_ref, k_ref, v_ref, col, n, rows)
                if last:
                    l_n = pltpu.roll(l_new, hd, 1)
                    acc_ref[rows, :] = acc_new / l_n
                else:
                    acc_ref[rows, :] = acc_new
                    m_ref[rows, :] = m_new
                    l_ref[rows, :] = l_new
                return carry

            lax.fori_loop(0, nblk, body, 0)
    o_ref[0, 0] = acc_ref[...].astype(o_ref.dtype)


def _dilated_attention(q, k, v):
    pairs, b, s, _ = q.shape
    ins, specs = [], []
    for _, dil in ATT_PATTERNS:
        for t in (q, k, v):
            ins.append(t.reshape(pairs, b, s // dil, dil * LANES))
            specs.append(pl.BlockSpec((1, 1, s // dil, dil * LANES), lambda i, j: (j, i, 0, 0)))
    return pl.pallas_call(
        functools.partial(_attn_kernel, s),
        out_shape=jax.ShapeDtypeStruct((pairs, b, s, LANES), BF16),
        grid=(b, pairs),
        in_specs=specs,
        out_specs=pl.BlockSpec((1, 1, s, LANES), lambda i, j: (j, i, 0, 0)),
        scratch_shapes=[pltpu.VMEM((s, LANES), F32)] * 3,
        compiler_params=_params("arbitrary", "arbitrary"),
        name="dilated_attention",
    )(*ins)


def _outproj_kernel(x_ref, mod_ref, mix_ref, att_ref, w_ref, o_ref):
    mw = mix_ref.shape[-1]
    y = _dot(mix_ref[0], w_ref[0:mw, :])
    for p in range(att_ref.shape[0]):
        y = y + _dot(att_ref[p, 0], w_ref[mw + p * LANES:mw + (p + 1) * LANES, :])
    o_ref[0] = x_ref[0] + mod_ref[0, 2:3, :] * y


def _out_projection(x, mod, mix, att, w_out, tm):
    b, s, d = x.shape
    pairs = att.shape[0]
    row = lambda i, j: (i, j, 0)
    return pl.pallas_call(
        _outproj_kernel,
        out_shape=jax.ShapeDtypeStruct((b, s, d), F32),
        grid=(b, s // tm),
        in_specs=[
            pl.BlockSpec((1, tm, d), row),
            pl.BlockSpec((1, SUBLANES, d), lambda i, j: (i, 0, 0)),
            pl.BlockSpec((1, tm, mix.shape[-1]), row),
            pl.BlockSpec((pairs, 1, tm, LANES), lambda i, j: (0, i, j, 0)),
            pl.BlockSpec(w_out.shape, lambda i, j: (0, 0)),
        ],
        out_specs=pl.BlockSpec((1, tm, d), row),
        compiler_params=_params("arbitrary", "arbitrary"),
        name="out_projection",
    )(x, mod, mix, att, w_out)


def _ffn_kernel(cfg, x_ref, mod_ref, g_ref, up_ref, cw_ref, cb_ref, down_ref, fg_ref, o_ref,
                hbuf, carry, acc):
    ffn, fc, final_norm = cfg
    tm = x_ref.shape[1]
    halo = SUBLANES
    x = x_ref[0]
    h = _modulated_norm(x, g_ref[...], mod_ref[0, 3:4, :], mod_ref[0, 4:5, :]).astype(BF16)

    @pl.when(pl.program_id(1) == 0)
    def _():
        carry[...] = jnp.zeros(carry.shape, F32)

    def conv_act(c0):
        hid = _dot(h, up_ref[:, c0:c0 + fc])
        hbuf[0:halo, :] = carry[:, c0:c0 + fc]
        hbuf[halo:halo + tm, :] = hid
        carry[:, c0:c0 + fc] = hid[tm - halo:tm, :]
        out = cb_ref[:, c0:c0 + fc]
        for k in range(FFN_CONV_K):
            off = halo - (FFN_CONV_K - 1) + k
            out = out + cw_ref[k:k + 1, c0:c0 + fc] * hbuf[off:off + tm, :]
        return out

    for j in range(ffn // fc):
        gate = conv_act(j * fc)
        val = conv_act(ffn + j * fc)
        part = _dot((_silu(gate) * val).astype(BF16), down_ref[j * fc:(j + 1) * fc, :])
        if j == 0:
            acc[...] = part
        else:
            acc[...] += part
    y = x + mod_ref[0, 5:6, :] * acc[...]
    if final_norm:
        y = y * lax.rsqrt(jnp.mean(y * y, axis=-1, keepdims=True) + NORM_EPS) * fg_ref[...]
    o_ref[0] = y


def _conv_ffn(x, mod, gain, up, cw, cb, down, final_g, final_norm, tm, fc):
    b, s, d = x.shape
    ffn = down.shape[0]
    row = lambda i, j: (i, j, 0)
    fixed2 = lambda i, j: (0, 0)
    full = lambda a: pl.BlockSpec(a.shape, fixed2)
    return pl.pallas_call(
        functools.partial(_ffn_kernel, (ffn, fc, final_norm)),
        out_shape=jax.ShapeDtypeStruct((b, s, d), F32),
        grid=(b, s // tm),
        in_specs=[
            pl.BlockSpec((1, tm, d), row),
            pl.BlockSpec((1, SUBLANES, d), lambda i, j: (i, 0, 0)),
            full(gain), full(up), full(cw), full(cb), full(down), full(final_g),
        ],
        out_specs=pl.BlockSpec((1, tm, d), row),
        scratch_shapes=[
            pltpu.VMEM((SUBLANES + tm, fc), F32),
            pltpu.VMEM((SUBLANES, 2 * ffn), F32),
            pltpu.VMEM((tm, d), F32),
        ],
        compiler_params=_params("arbitrary", "arbitrary"),
        name="conv_ffn",
    )(x, mod, gain, up, cw, cb, down, final_g)


def _pad_lanes(v, width=LANES):
    return jnp.pad(v, ((0, 0), (0, width - v.shape[-1])))


def kernel(x, c, positions, ada_w, ada_b, norm1_g, w_in, ssd_conv_w, ssd_conv_b, ssd_dt_bias, ssd_a_log,
           ssd_d, ssd_norm_g, pool_w, pool_scale, w_out, norm2_g, ffn_up, ffn_conv_w, ffn_conv_b, ffn_down,
           final_g):
    depth = w_in.shape[0]
    b, s, d = x.shape
    heads = ssd_dt_bias.shape[-1]
    inner = heads * SSD_HEAD_DIM
    conv_ch = ssd_conv_w.shape[-1]
    pool_width = pool_scale.shape[-1]
    ssd_proj = inner + conv_ch + heads
    att_w = (w_in.shape[-1] - ssd_proj - pool_width) // 3
    widths = (inner, conv_ch, pool_width, att_w)
    assert conv_ch == inner + 2 * SSD_GROUPS * SSD_STATE and att_w % LANES == 0
    assert s % (ATT_BLOCK * ATT_PATTERNS[-1][1]) == 0 and s % SSD_CHUNK == 0
    tm = 512

    mod_all = _ada_mod(c, ada_w, ada_b).reshape(depth, b, 6, d)
    mod_all = jnp.pad(mod_all, ((0, 0), (0, 0), (0, SUBLANES - 6), (0, 0)))
    cos, sin = _rope_tables(positions)
    a0, a1 = ssd_proj, ssd_proj + pool_width

    for i in range(depth):
        mod = mod_all[i]
        wi = w_in[i]
        w_main = jnp.concatenate([wi[:, :inner + conv_ch], wi[:, a0:]], axis=1).astype(BF16)
        w_dt = _pad_lanes(wi[:, inner + conv_ch:a0]).astype(BF16)
        z, xbc, dt, u, q, k, v = _in_projection(x, mod, norm1_g[i][None], w_main, w_dt, cos, sin, widths, tm)
        pw_bd = jax.scipy.linalg.block_diag(*[pool_w[i, g] for g in range(pool_w.shape[1])]).astype(BF16)
        mix = _ssd_pool(
            z, xbc, dt, u, ssd_conv_w[i], ssd_conv_b[i][None], _pad_lanes(ssd_dt_bias[i][None]),
            _pad_lanes(ssd_a_log[i][None]), jnp.repeat(ssd_d[i], SSD_HEAD_DIM)[None], ssd_norm_g[i][None],
            pw_bd, pool_scale[i][None], heads)
        att = _dilated_attention(q, k, v)
        x = _out_projection(x, mod, mix, att, w_out[i].astype(BF16), tm)
        x = _conv_ffn(x, mod, norm2_g[i][None], ffn_up[i].astype(BF16), ffn_conv_w[i], ffn_conv_b[i][None],
                      ffn_down[i].astype(BF16), final_g[None], i == depth - 1, tm, 256)
    return x
```

```python
import functools
import math

import jax
import jax.numpy as jnp
from jax import lax
from jax.experimental import pallas as pl
from jax.experimental.pallas import tpu as pltpu

F32 = jnp.float32
BF16 = jnp.bfloat16

LANES = 128
SUBLANES = 8
VMEM_LIMIT = 56 * 1024 * 1024

NORM_EPS = 1e-6
ROPE_THETA = 500000.0

SSD_HEAD_DIM = 64
SSD_GROUPS = 2
SSD_STATE = 128
SSD_CONV_K = 4
SSD_CHUNK = 256
POOL_WINDOWS = (2, 4, 8, 16)
ATT_HEAD_DIM = 64
ATT_PATTERNS = ((128, 1), (512, 4), (2048, 16))
ATT_BLOCK = 128
ROT_DIM = ATT_HEAD_DIM // 4
FFN_CONV_K = 3

CONV_HALO = SUBLANES
POOL_HALO = 2 * SUBLANES


def _params(*semantics):
    return pltpu.CompilerParams(dimension_semantics=semantics, vmem_limit_bytes=VMEM_LIMIT)


def _silu(v):
    return v * jax.nn.sigmoid(v)


def _dot(a, b):
    return jnp.dot(a, b, preferred_element_type=F32)


def _dot_nt(a, b):
    return lax.dot_general(a, b, (((1,), (1,)), ((), ())), preferred_element_type=F32)


def _dot_tn(a, b):
    return lax.dot_general(a, b, (((0,), (0,)), ((), ())), preferred_element_type=F32)


def _ada_kernel(c_ref, w_ref, b_ref, o_ref):
    c_act = _silu(c_ref[...])
    o_ref[0] = jnp.dot(c_act, w_ref[0], preferred_element_type=F32,
                       precision=lax.Precision.HIGHEST) + b_ref[0]


def _ada_mod(c, ada_w, ada_b):
    depth, d, n = ada_w.shape
    b = c.shape[0]
    tn = d
    return pl.pallas_call(
        _ada_kernel,
        out_shape=jax.ShapeDtypeStruct((depth, b, n), F32),
        grid=(depth, n // tn),
        in_specs=[
            pl.BlockSpec((b, d), lambda i, j: (0, 0)),
            pl.BlockSpec((1, d, tn), lambda i, j: (i, 0, j)),
            pl.BlockSpec((1, 1, tn), lambda i, j: (i, 0, j)),
        ],
        out_specs=pl.BlockSpec((1, b, tn), lambda i, j: (i, 0, j)),
        compiler_params=_params("arbitrary", "arbitrary"),
        name="ada_mod",
    )(c, ada_w, ada_b.reshape(depth, 1, n))


def _rope_kernel(pos_ref, freq_ref, sign_ref, cos_ref, sin_ref):
    ang = pos_ref[0] * freq_ref[...]
    cos_ref[0] = jnp.cos(ang)
    sin_ref[0] = jnp.sin(ang) * sign_ref[...]


def _rope_tables(positions):
    b, s = positions.shape
    half = ROT_DIM // 2
    inv_freq = ROPE_THETA ** (-jnp.arange(0, ROT_DIM, 2, dtype=F32) / ROT_DIM)
    lane = jnp.arange(LANES) % ATT_HEAD_DIM
    freq = jnp.where(lane < ROT_DIM, inv_freq[lane % half], 0.0).astype(F32)[None, :]
    sign = jnp.where(lane < half, -1.0, jnp.where(lane < ROT_DIM, 1.0, 0.0)).astype(F32)[None, :]
    tm = min(s, 1024)
    return pl.pallas_call(
        _rope_kernel,
        out_shape=(jax.ShapeDtypeStruct((b, s, LANES), F32),) * 2,
        grid=(b, s // tm),
        in_specs=[
            pl.BlockSpec((1, tm, 1), lambda i, j: (i, j, 0)),
            pl.BlockSpec((1, LANES), lambda i, j: (0, 0)),
            pl.BlockSpec((1, LANES), lambda i, j: (0, 0)),
        ],
        out_specs=(pl.BlockSpec((1, tm, LANES), lambda i, j: (i, j, 0)),) * 2,
        compiler_params=_params("arbitrary", "arbitrary"),
        name="rope_tables",
    )(positions.astype(F32)[..., None], freq, sign)


def _modulated_norm(x, gain, shift, scale):
    y = x * lax.rsqrt(jnp.mean(x * x, axis=-1, keepdims=True) + NORM_EPS)
    return (y * gain) * (1.0 + scale) + shift


def _rotate(t, cos, sin):
    half = ROT_DIM // 2
    lane = lax.broadcasted_iota(jnp.int32, t.shape, 1) % ATT_HEAD_DIM
    partner = jnp.where(lane < half, pltpu.roll(t, LANES - half, 1), pltpu.roll(t, half, 1))
    return t * cos + partner * sin


def _inproj_kernel(widths, x_ref, mod_ref, g_ref, w_ref, wdt_ref, cos_ref, sin_ref,
                   z_ref, xbc_ref, dt_ref, u_ref, q_ref, k_ref, v_ref):
    inner, conv_ch, pool_w, att_w = widths
    h = _modulated_norm(x_ref[0], g_ref[...], mod_ref[0, 0:1, :], mod_ref[0, 1:2, :]).astype(BF16)
    o = 0
    z_ref[0] = _dot(h, w_ref[:, o:o + inner])
    o += inner
    xbc_ref[0] = _dot(h, w_ref[:, o:o + conv_ch])
    o += conv_ch
    u_ref[0] = _dot(h, w_ref[:, o:o + pool_w])
    o += pool_w
    dt_ref[0] = _dot(h, wdt_ref[...])
    cos, sin = cos_ref[0], sin_ref[0]
    scale = ATT_HEAD_DIM ** -0.5
    for p in range(att_w // LANES):
        q = _dot(h, w_ref[:, o + p * LANES:o + (p + 1) * LANES])
        q_ref[p, 0] = (_rotate(q, cos, sin) * scale).astype(BF16)
        k = _dot(h, w_ref[:, o + att_w + p * LANES:o + att_w + (p + 1) * LANES])
        k_ref[p, 0] = _rotate(k, cos, sin).astype(BF16)
        v = _dot(h, w_ref[:, o + 2 * att_w + p * LANES:o + 2 * att_w + (p + 1) * LANES])
        v_ref[p, 0] = v.astype(BF16)


def _in_projection(x, mod, gain, w_main, w_dt, cos, sin, widths, tm):
    b, s, d = x.shape
    inner, conv_ch, pool_w, att_w = widths
    pairs = att_w // LANES
    row = lambda i, j: (i, j, 0)
    fixed2 = lambda i, j: (0, 0)
    pair_spec = pl.BlockSpec((pairs, 1, tm, LANES), lambda i, j: (0, i, j, 0))
    pair_shape = jax.ShapeDtypeStruct((pairs, b, s, LANES), BF16)
    return pl.pallas_call(
        functools.partial(_inproj_kernel, widths),
        out_shape=(
            jax.ShapeDtypeStruct((b, s, inner), F32),
            jax.ShapeDtypeStruct((b, s, conv_ch), F32),
            jax.ShapeDtypeStruct((b, s, LANES), F32),
            jax.ShapeDtypeStruct((b, s, pool_w), F32),
            pair_shape, pair_shape, pair_shape,
        ),
        grid=(b, s // tm),
        in_specs=[
            pl.BlockSpec((1, tm, d), row),
            pl.BlockSpec((1, SUBLANES, d), lambda i, j: (i, 0, 0)),
            pl.BlockSpec((1, d), fixed2),
            pl.BlockSpec(w_main.shape, fixed2),
            pl.BlockSpec(w_dt.shape, fixed2),
            pl.BlockSpec((1, tm, LANES), row),
            pl.BlockSpec((1, tm, LANES), row),
        ],
        out_specs=(
            pl.BlockSpec((1, tm, inner), row),
            pl.BlockSpec((1, tm, conv_ch), row),
            pl.BlockSpec((1, tm, LANES), row),
            pl.BlockSpec((1, tm, pool_w), row),
            pair_spec, pair_spec, pair_spec,
        ),
        compiler_params=_params("arbitrary", "arbitrary"),
        name="in_projection",
    )(x, mod, gain, w_main, w_dt, cos, sin)


def _pair_rep(col, h0):
    lane = lax.broadcasted_iota(jnp.int32, (col.shape[0], LANES), 1)
    return jnp.where(lane < SSD_HEAD_DIM, col[:, h0:h0 + 1], col[:, h0 + 1:h0 + 2])


def _ssd_pool_kernel(dims, z_ref, xbc_ref, dt_ref, u_ref, cw_ref, cb_ref, dtb_ref, alog_ref, dskip_ref,
                     ng_ref, pw_ref, ps_ref, o_ref, xbuf, ubuf, hstate):
    inner, heads = dims
    q = SSD_CHUNK
    gw = inner // SSD_GROUPS
    n = SSD_STATE
    chunk = pl.program_id(1)

    @pl.when(chunk == 0)
    def _():
        xbuf[0:CONV_HALO, :] = jnp.zeros((CONV_HALO, xbuf.shape[1]), F32)
        ubuf[0:POOL_HALO, :] = jnp.zeros((POOL_HALO, ubuf.shape[1]), F32)
        hstate[...] = jnp.zeros(hstate.shape, F32)

    xbuf[CONV_HALO:CONV_HALO + q, :] = xbc_ref[0]
    conv = cb_ref[...]
    for k in range(SSD_CONV_K):
        off = CONV_HALO - (SSD_CONV_K - 1) + k
        conv = conv + cw_ref[k:k + 1, :] * xbuf[off:off + q, :]
    xbc = _silu(conv)
    xbuf[0:CONV_HALO, :] = xbuf[q:q + CONV_HALO, :]

    lane = lax.broadcasted_iota(jnp.int32, (q, LANES), 1)
    dt = jax.nn.softplus(dt_ref[0] + dtb_ref[...])
    a = -jnp.exp(alog_ref[...])
    da = jnp.where(lane < heads, dt * a, 0.0)
    rows = lax.broadcasted_iota(jnp.int32, (q, q), 0)
    cols = lax.broadcasted_iota(jnp.int32, (q, q), 1)
    causal = rows >= cols
    tri = jnp.where(causal, 1.0, 0.0).astype(F32)
    a_col = jnp.dot(tri, da, preferred_element_type=F32, precision=lax.Precision.HIGHEST)
    a_row = a_col.T

    pair_lane = lax.broadcasted_iota(jnp.int32, (q, LANES), 1)
    left = pair_lane < SSD_HEAD_DIM
    hpg = heads // SSD_GROUPS
    y_groups = []
    for g in range(SSD_GROUPS):
        xs = xbc[:, g * gw:(g + 1) * gw]
        bm = xbc[:, inner + g * n:inner + (g + 1) * n].astype(BF16)
        cm = xbc[:, inner + SSD_GROUPS * n + g * n:inner + SSD_GROUPS * n + (g + 1) * n].astype(BF16)
        cb = _dot_nt(cm, bm)
        y_off = _dot(cm, hstate[g].astype(BF16))
        y_pairs, xd_pairs, cd_pairs = [], [], []
        for p in range(hpg // 2):
            h0 = g * hpg + 2 * p
            xs_p = xs[:, p * LANES:(p + 1) * LANES]
            acum = _pair_rep(a_col, h0)
            xdt = xs_p * _pair_rep(dt, h0)
            y_p = y_off[:, p * LANES:(p + 1) * LANES] * jnp.exp(acum)
            for side in range(2):
                hh = h0 + side
                seg = a_col[:, hh:hh + 1] - a_row[hh:hh + 1, :]
                lmat = jnp.exp(jnp.where(causal, seg, -jnp.inf))
                keep = left if side == 0 else jnp.logical_not(left)
                y_p = y_p + _dot((cb * lmat).astype(BF16), jnp.where(keep, xdt, 0.0).astype(BF16))
            a_last = acum[q - 1:q, :]
            xd_pairs.append((xdt * jnp.exp(a_last - acum)).astype(BF16))
            cd_pairs.append(jnp.exp(a_last))
            y_pairs.append(y_p + dskip_ref[:, h0 * SSD_HEAD_DIM:(h0 + 2) * SSD_HEAD_DIM] * xs_p)
        xd = jnp.concatenate(xd_pairs, axis=1)
        hstate[g] = hstate[g] * jnp.concatenate(cd_pairs, axis=1) + _dot_tn(bm, xd)
        y = jnp.concatenate(y_pairs, axis=1) * _silu(z_ref[0, :, g * gw:(g + 1) * gw])
        y = y * lax.rsqrt(jnp.mean(y * y, axis=-1, keepdims=True) + NORM_EPS)
        y_groups.append(y * ng_ref[:, g * gw:(g + 1) * gw])
    o_ref[0, :, 0:inner] = jnp.concatenate(y_groups, axis=1).astype(o_ref.dtype)

    pool_w = ubuf.shape[1]
    pool_ch = pool_w // len(POOL_WINDOWS)
    u = u_ref[0]
    ubuf[POOL_HALO:POOL_HALO + q, :] = u
    plane = lax.broadcasted_iota(jnp.int32, (q, pool_w), 1)
    pos = chunk * q + lax.broadcasted_iota(jnp.int32, (q, pool_w), 0)
    acc = u
    pooled = jnp.zeros((q, pool_w), F32)
    win = jnp.zeros((q, pool_w), jnp.int32)
    done = 1
    for gi, w in enumerate(POOL_WINDOWS):
        for k in range(done, w):
            acc = acc + ubuf[POOL_HALO - k:POOL_HALO - k + q, :]
        done = w
        in_group = (plane >= gi * pool_ch) & (plane < (gi + 1) * pool_ch)
        pooled = jnp.where(in_group, acc, pooled)
        win = jnp.where(in_group, w, win)
    cnt = jnp.minimum(pos + 1, win).astype(F32)
    diff = (pooled / cnt - u).astype(BF16)
    ubuf[0:POOL_HALO, :] = ubuf[q:q + POOL_HALO, :]
    o_ref[0, :, inner:inner + pool_w] = (_dot(diff, pw_ref[...]) * ps_ref[...]).astype(o_ref.dtype)


def _ssd_pool(z, xbc, dt, u, cw, cb, dtb, alog, dskip, ng, pw, ps, heads):
    b, s, inner = z.shape
    conv_ch = xbc.shape[-1]
    pool_w = u.shape[-1]
    q = SSD_CHUNK
    row = lambda i, j: (i, j, 0)
    fixed2 = lambda i, j: (0, 0)
    full = lambda a: pl.BlockSpec(a.shape, fixed2)
    out_w = inner + pool_w
    return pl.pallas_call(
        functools.partial(_ssd_pool_kernel, (inner, heads)),
        out_shape=jax.ShapeDtypeStruct((b, s, out_w), BF16),
        grid=(b, s // q),
        in_specs=[
            pl.BlockSpec((1, q, inner), row),
            pl.BlockSpec((1, q, conv_ch), row),
            pl.BlockSpec((1, q, LANES), row),
            pl.BlockSpec((1, q, pool_w), row),
            full(cw), full(cb), full(dtb), full(alog), full(dskip), full(ng), full(pw), full(ps),
        ],
        out_specs=pl.BlockSpec((1, q, out_w), row),
        scratch_shapes=[
            pltpu.VMEM((CONV_HALO + q, conv_ch), F32),
            pltpu.VMEM((POOL_HALO + q, pool_w), F32),
            pltpu.VMEM((SSD_GROUPS, SSD_STATE, inner // SSD_GROUPS), F32),
        ],
        compiler_params=_params("arbitrary", "arbitrary"),
        name="ssd_pool",
    )(z, xbc, dt, u, cw, cb, dtb, alog, dskip, ng, pw, ps)


ATT_ORDER = tuple(sorted(ATT_PATTERNS, key=lambda wd: -wd[1]))
ATT_UNROLL_MAX = 4


def _attn_kernel(seq, *refs):
    nbr = len(ATT_ORDER)
    qkv_refs = refs[:3 * nbr]
    o_ref = refs[3 * nbr]
    acc_ref, l_ref, m0_ref, m1_ref = refs[3 * nbr + 1:]
    blk = ATT_BLOCK
    hd = ATT_HEAD_DIM
    lane = lax.broadcasted_iota(jnp.int32, (blk, LANES), 1)
    left = lane < hd
    qi = lax.broadcasted_iota(jnp.int32, (blk, blk), 0)
    kj = lax.broadcasted_iota(jnp.int32, (blk, blk), 1)
    mask_cur = kj <= qi
    mask_prev = kj >= qi
    neg = -jnp.inf

    keeps = (left, jnp.logical_not(left))
    left_band = lax.broadcasted_iota(jnp.int32, (2 * blk, LANES), 1) < hd
    keeps_band = (left_band, jnp.logical_not(left_band))

    def softmax_blocks(qs, kbands, vbands, pens, states):
        nb = len(qs)
        zero = jnp.zeros_like(qs[0])
        one = jnp.ones_like(vbands[0])
        scores = []
        for u in range(nb):
            for side in range(2):
                s = _dot_nt(jnp.where(keeps[side], qs[u], zero), kbands[u])
                s_p = s[:, :blk] if pens[u] is None else s[:, :blk] + pens[u]
                scores.append((jnp.where(mask_prev, s_p, neg), jnp.where(mask_cur, s[:, blk:], neg)))
        m_new = []
        for u in range(nb):
            for side in range(2):
                s_p, s_c = scores[2 * u + side]
                m_blk = jnp.max(jnp.maximum(s_p, s_c), axis=-1, keepdims=True)
                if states[u] is None:
                    m_new.append(jnp.broadcast_to(m_blk, (blk, LANES)))
                else:
                    m_new.append(jnp.maximum(states[u][side], m_blk))
        res = []
        for u in range(nb):
            for side in range(2):
                s_p, s_c = scores[2 * u + side]
                m = m_new[2 * u + side]
                p = jnp.concatenate([jnp.exp(s_p - m), jnp.exp(s_c - m)], axis=1).astype(BF16)
                res.append(_dot(p, jnp.where(keeps_band[side], vbands[u], one)))
        outs = []
        for u in range(nb):
            acc_new = jnp.where(left, res[2 * u], res[2 * u + 1])
            l_new = jnp.where(left, res[2 * u + 1], res[2 * u])
            if states[u] is not None:
                alpha = [jnp.exp(states[u][side] - m_new[2 * u + side]) for side in range(2)]
                acc_new = jnp.where(left, alpha[0], alpha[1]) * states[u][2] + acc_new
                l_new = jnp.where(left, alpha[1], alpha[0]) * states[u][3] + l_new
            outs.append((m_new[2 * u], m_new[2 * u + 1], acc_new, l_new))
        return outs

    for bi, (window, dil) in enumerate(ATT_ORDER):
        assert window // dil == blk
        q_ref, k_ref, v_ref = qkv_refs[3 * bi:3 * bi + 3]
        nblk = seq // dil // blk
        first = bi == 0
        last = bi == nbr - 1
        unroll = min(ATT_UNROLL_MAX, nblk)
        assert nblk % unroll == 0 and (not last or dil == 1)
        for r in range(dil):
            col = r * LANES

            def body(it, carry, first=first, last=last, dil=dil, r=r, col=col, unroll=unroll,
                     q_ref=q_ref, k_ref=k_ref, v_ref=v_ref):
                n0 = it * unroll

                def tile(ref, n):
                    if isinstance(n, int):
                        return ref[0, 0, n * blk:(n + 1) * blk, col:col + LANES]
                    return ref[0, 0, pl.ds(pl.multiple_of(n * blk, blk), blk), col:col + LANES]

                def state_rows(n):
                    if dil == 1:
                        return pl.ds(n * blk if isinstance(n, int) else pl.multiple_of(n * blk, blk), blk)
                    return pl.ds(n * (blk * dil) + r, blk, stride=dil)

                n_prev = max(n0 - 1, 0) if isinstance(n0, int) else jnp.maximum(n0 - 1, 0)
                ks = [tile(k_ref, n_prev)] + [tile(k_ref, n0 + u) for u in range(unroll)]
                vs = [tile(v_ref, n_prev)] + [tile(v_ref, n0 + u) for u in range(unroll)]
                qs = [tile(q_ref, n0 + u) for u in range(unroll)]
                rows = [state_rows(n0 + u) for u in range(unroll)]
                states = [None if first else (m0_ref[rw, :], m1_ref[rw, :], acc_ref[rw, :], l_ref[rw, :])
                          for rw in rows]
                if isinstance(n0, int):
                    pen0 = None if n0 > 0 else neg
                else:
                    pen0 = jnp.where(n0 > 0, 0.0, neg).astype(F32)
                kbands = [jnp.concatenate([ks[u], ks[u + 1]], axis=0) for u in range(unroll)]
                vbands = [jnp.concatenate([vs[u], vs[u + 1]], axis=0) for u in range(unroll)]
                outs = softmax_blocks(qs, kbands, vbands, [pen0] + [None] * (unroll - 1), states)
                for rw, (m0_new, m1_new, acc_new, l_new) in zip(rows, outs):
                    if last:
                        o_ref[0, 0, rw, :] = (acc_new / pltpu.roll(l_new, hd, 1)).astype(o_ref.dtype)
                    else:
                        m0_ref[rw, :] = m0_new
                        m1_ref[rw, :] = m1_new
                        acc_ref[rw, :] = acc_new
                        l_ref[rw, :] = l_new
                return carry

            if nblk == unroll:
                body(0, 0)
            else:
                lax.fori_loop(0, nblk // unroll, body, 0)


def _dilated_attention(q, k, v):
    pairs, b, s, _ = q.shape
    ins, specs = [], []
    for _, dil in ATT_ORDER:
        for t in (q, k, v):
            ins.append(t.reshape(pairs, b, s // dil, dil * LANES))
            specs.append(pl.BlockSpec((1, 1, s // dil, dil * LANES), lambda i, j: (j, i, 0, 0)))
    return pl.pallas_call(
        functools.partial(_attn_kernel, s),
        out_shape=jax.ShapeDtypeStruct((pairs, b, s, LANES), BF16),
        grid=(b, pairs),
        in_specs=specs,
        out_specs=pl.BlockSpec((1, 1, s, LANES), lambda i, j: (j, i, 0, 0)),
        scratch_shapes=[pltpu.VMEM((s, LANES), F32)] * 4,
        compiler_params=_params("arbitrary", "arbitrary"),
        name="dilated_attention",
    )(*ins)


def _outproj_kernel(x_ref, mod_ref, mix_ref, att_ref, w_ref, o_ref):
    mw = mix_ref.shape[-1]
    y = _dot(mix_ref[0], w_ref[0:mw, :])
    for p in range(att_ref.shape[0]):
        y = y + _dot(att_ref[p, 0], w_ref[mw + p * LANES:mw + (p + 1) * LANES, :])
    o_ref[0] = x_ref[0] + mod_ref[0, 2:3, :] * y


def _out_projection(x, mod, mix, att, w_out, tm):
    b, s, d = x.shape
    pairs = att.shape[0]
    row = lambda i, j: (i, j, 0)
    return pl.pallas_call(
        _outproj_kernel,
        out_shape=jax.ShapeDtypeStruct((b, s, d), F32),
        grid=(b, s // tm),
        in_specs=[
            pl.BlockSpec((1, tm, d), row),
            pl.BlockSpec((1, SUBLANES, d), lambda i, j: (i, 0, 0)),
            pl.BlockSpec((1, tm, mix.shape[-1]), row),
            pl.BlockSpec((pairs, 1, tm, LANES), lambda i, j: (0, i, j, 0)),
            pl.BlockSpec(w_out.shape, lambda i, j: (0, 0)),
        ],
        out_specs=pl.BlockSpec((1, tm, d), row),
        compiler_params=_params("arbitrary", "arbitrary"),
        name="out_projection",
    )(x, mod, mix, att, w_out)


def _ffn_kernel(cfg, x_ref, mod_ref, g_ref, up_ref, cw_ref, cb_ref, down_ref, fg_ref, o_ref,
                hbuf, carry, acc):
    ffn, fc, final_norm = cfg
    tm = x_ref.shape[1]
    halo = SUBLANES
    x = x_ref[0]
    h = _modulated_norm(x, g_ref[...], mod_ref[0, 3:4, :], mod_ref[0, 4:5, :]).astype(BF16)

    @pl.when(pl.program_id(1) == 0)
    def _():
        carry[...] = jnp.zeros(carry.shape, F32)

    def conv_act(c0):
        hid = _dot(h, up_ref[:, c0:c0 + fc])
        hbuf[0:halo, :] = carry[:, c0:c0 + fc]
        hbuf[halo:halo + tm, :] = hid
        carry[:, c0:c0 + fc] = hid[tm - halo:tm, :]
        out = cb_ref[:, c0:c0 + fc]
        for k in range(FFN_CONV_K):
            off = halo - (FFN_CONV_K - 1) + k
            out = out + cw_ref[k:k + 1, c0:c0 + fc] * hbuf[off:off + tm, :]
        return out

    for j in range(ffn // fc):
        gate = conv_act(j * fc)
        val = conv_act(ffn + j * fc)
        part = _dot((_silu(gate) * val).astype(BF16), down_ref[j * fc:(j + 1) * fc, :])
        if j == 0:
            acc[...] = part
        else:
            acc[...] += part
    y = x + mod_ref[0, 5:6, :] * acc[...]
    if final_norm:
        y = y * lax.rsqrt(jnp.mean(y * y, axis=-1, keepdims=True) + NORM_EPS) * fg_ref[...]
    o_ref[0] = y


def _conv_ffn(x, mod, gain, up, cw, cb, down, final_g, final_norm, tm, fc):
    b, s, d = x.shape
    ffn = down.shape[0]
    row = lambda i, j: (i, j, 0)
    fixed2 = lambda i, j: (0, 0)
    full = lambda a: pl.BlockSpec(a.shape, fixed2)
    return pl.pallas_call(
        functools.partial(_ffn_kernel, (ffn, fc, final_norm)),
        out_shape=jax.ShapeDtypeStruct((b, s, d), F32),
        grid=(b, s // tm),
        in_specs=[
            pl.BlockSpec((1, tm, d), row),
            pl.BlockSpec((1, SUBLANES, d), lambda i, j: (i, 0, 0)),
            full(gain), full(up), full(cw), full(cb), full(down), full(final_g),
        ],
        out_specs=pl.BlockSpec((1, tm, d), row),
        scratch_shapes=[
            pltpu.VMEM((SUBLANES + tm, fc), F32),
            pltpu.VMEM((SUBLANES, 2 * ffn), F32),
            pltpu.VMEM((tm, d), F32),
        ],
        compiler_params=_params("arbitrary", "arbitrary"),
        name="conv_ffn",
    )(x, mod, gain, up, cw, cb, down, final_g)


def _pad_lanes(v, width=LANES):
    return jnp.pad(v, ((0, 0), (0, width - v.shape[-1])))


def kernel(x, c, positions, ada_w, ada_b, norm1_g, w_in, ssd_conv_w, ssd_conv_b, ssd_dt_bias, ssd_a_log,
           ssd_d, ssd_norm_g, pool_w, pool_scale, w_out, norm2_g, ffn_up, ffn_conv_w, ffn_conv_b, ffn_down,
           final_g):
    depth = w_in.shape[0]
    b, s, d = x.shape
    heads = ssd_dt_bias.shape[-1]
    inner = heads * SSD_HEAD_DIM
    conv_ch = ssd_conv_w.shape[-1]
    pool_width = pool_scale.shape[-1]
    ssd_proj = inner + conv_ch + heads
    att_w = (w_in.shape[-1] - ssd_proj - pool_width) // 3
    widths = (inner, conv_ch, pool_width, att_w)
    assert conv_ch == inner + 2 * SSD_GROUPS * SSD_STATE and att_w % LANES == 0
    assert s % (ATT_BLOCK * ATT_PATTERNS[-1][1]) == 0 and s % SSD_CHUNK == 0
    tm = 512

    mod_all = _ada_mod(c, ada_w, ada_b).reshape(depth, b, 6, d)
    mod_all = jnp.pad(mod_all, ((0, 0), (0, 0), (0, SUBLANES - 6), (0, 0)))
    cos, sin = _rope_tables(positions)
    a0, a1 = ssd_proj, ssd_proj + pool_width

    for i in range(depth):
        mod = mod_all[i]
        wi = w_in[i]
        w_main = jnp.concatenate([wi[:, :inner + conv_ch], wi[:, a0:]], axis=1).astype(BF16)
        w_dt = _pad_lanes(wi[:, inner + conv_ch:a0]).astype(BF16)
        z, xbc, dt, u, q, k, v = _in_projection(x, mod, norm1_g[i][None], w_main, w_dt, cos, sin, widths, tm)
        pw_bd = jax.scipy.linalg.block_diag(*[pool_w[i, g] for g in range(pool_w.shape[1])]).astype(BF16)
        mix = _ssd_pool(
            z, xbc, dt, u, ssd_conv_w[i], ssd_conv_b[i][None], _pad_lanes(ssd_dt_bias[i][None]),
            _pad_lanes(ssd_a_log[i][None]), jnp.repeat(ssd_d[i], SSD_HEAD_DIM)[None], ssd_norm_g[i][None],
            pw_bd, pool_scale[i][None], heads)
        att = _dilated_attention(q, k, v)
        x = _out_projection(x, mod, mix, att, w_out[i].astype(BF16), tm)
        x = _conv_ffn(x, mod, norm2_g[i][None], ffn_up[i].astype(BF16), ffn_conv_w[i], ffn_conv_b[i][None],
                      ffn_down[i].astype(BF16), final_g[None], i == depth - 1, tm, 256)
    return x
```

```python
import functools

import jax
import jax.numpy as jnp
from jax import lax
from jax.experimental import pallas as pl
from jax.experimental.pallas import tpu as pltpu

F32 = jnp.float32
BF16 = jnp.bfloat16

LANES = 128
SUBLANES = 8
VMEM_LIMIT = 56 * 1024 * 1024

NORM_EPS = 1e-6
ROPE_THETA = 500000.0

SSD_HEAD_DIM = 64
SSD_GROUPS = 2
SSD_STATE = 128
SSD_CONV_K = 4
SSD_CHUNK = 256
POOL_WINDOWS = (2, 4, 8, 16)
ATT_HEAD_DIM = 64
ATT_PATTERNS = ((128, 1), (512, 4), (2048, 16))
ATT_BLOCK = 128
ATT_ORDER = tuple(sorted(ATT_PATTERNS, key=lambda wd: -wd[1]))
ATT_UNROLL_MAX = 4
ROT_DIM = ATT_HEAD_DIM // 4
FFN_CONV_K = 3

CONV_HALO = SUBLANES
POOL_HALO = 2 * SUBLANES


def _params(*semantics):
    return pltpu.CompilerParams(dimension_semantics=semantics, vmem_limit_bytes=VMEM_LIMIT)


def _resident(a):
    return pl.BlockSpec(a.shape, lambda *_: (0,) * a.ndim, pipeline_mode=pl.Buffered(1))


def _silu(v):
    return v * jax.nn.sigmoid(v)


def _dot(a, b):
    return jnp.dot(a, b, preferred_element_type=F32)


def _dot_nt(a, b):
    return lax.dot_general(a, b, (((1,), (1,)), ((), ())), preferred_element_type=F32)


def _ada_kernel(c_ref, w_ref, b_ref, o_ref):
    c_act = _silu(c_ref[...])
    o_ref[0] = jnp.dot(c_act, w_ref[0], preferred_element_type=F32,
                       precision=lax.Precision.HIGHEST) + b_ref[0]


def _ada_mod(c, ada_w, ada_b):
    depth, d, n = ada_w.shape
    b = c.shape[0]
    tn = d
    return pl.pallas_call(
        _ada_kernel,
        out_shape=jax.ShapeDtypeStruct((depth, b, n), F32),
        grid=(depth, n // tn),
        in_specs=[
            pl.BlockSpec((b, d), lambda i, j: (0, 0)),
            pl.BlockSpec((1, d, tn), lambda i, j: (i, 0, j)),
            pl.BlockSpec((1, 1, tn), lambda i, j: (i, 0, j)),
        ],
        out_specs=pl.BlockSpec((1, b, tn), lambda i, j: (i, 0, j)),
        compiler_params=_params("arbitrary", "arbitrary"),
        name="ada_mod",
    )(c, ada_w, ada_b.reshape(depth, 1, n))


def _rope_kernel(pos_ref, freq_ref, sign_ref, cos_ref, sin_ref):
    ang = pos_ref[0] * freq_ref[...]
    cos_ref[0] = jnp.cos(ang)
    sin_ref[0] = jnp.sin(ang) * sign_ref[...]


def _rope_tables(positions):
    b, s = positions.shape
    half = ROT_DIM // 2
    inv_freq = ROPE_THETA ** (-jnp.arange(0, ROT_DIM, 2, dtype=F32) / ROT_DIM)
    lane = jnp.arange(LANES) % ATT_HEAD_DIM
    freq = jnp.where(lane < ROT_DIM, inv_freq[lane % half], 0.0).astype(F32)[None, :]
    sign = jnp.where(lane < half, -1.0, jnp.where(lane < ROT_DIM, 1.0, 0.0)).astype(F32)[None, :]
    tm = min(s, 1024)
    return pl.pallas_call(
        _rope_kernel,
        out_shape=(jax.ShapeDtypeStruct((b, s, LANES), F32),) * 2,
        grid=(b, s // tm),
        in_specs=[
            pl.BlockSpec((1, tm, 1), lambda i, j: (i, j, 0)),
            pl.BlockSpec((1, LANES), lambda i, j: (0, 0)),
            pl.BlockSpec((1, LANES), lambda i, j: (0, 0)),
        ],
        out_specs=(pl.BlockSpec((1, tm, LANES), lambda i, j: (i, j, 0)),) * 2,
        compiler_params=_params("arbitrary", "arbitrary"),
        name="rope_tables",
    )(positions.astype(F32)[..., None], freq, sign)


def _modulated_norm(x, gain, shift, scale):
    y = x * lax.rsqrt(jnp.mean(x * x, axis=-1, keepdims=True) + NORM_EPS)
    return (y * gain) * (1.0 + scale) + shift


def _rotate(t, cos, sin):
    half = ROT_DIM // 2
    lane = lax.broadcasted_iota(jnp.int32, t.shape, 1) % ATT_HEAD_DIM
    partner = jnp.where(lane < half, pltpu.roll(t, LANES - half, 1), pltpu.roll(t, half, 1))
    return t * cos + partner * sin


def _inproj_kernel(widths, x_ref, mod_ref, g_ref, w_ref, wdt_ref, cos_ref, sin_ref,
                   z_ref, xbc_ref, dt_ref, u_ref, *rest):
    qkv_refs, slabs = rest[:-1], rest[-1]
    inner, conv_ch, pool_w, att_w = widths
    tm = x_ref.shape[1]
    h = _modulated_norm(x_ref[0], g_ref[...], mod_ref[0, 0:1, :], mod_ref[0, 1:2, :]).astype(BF16)
    o = 0
    z_ref[0] = _dot(h, w_ref[:, o:o + inner]).astype(z_ref.dtype)
    o += inner
    xbc_ref[0] = _dot(h, w_ref[:, o:o + conv_ch]).astype(xbc_ref.dtype)
    o += conv_ch
    u_ref[0] = _dot(h, w_ref[:, o:o + pool_w]).astype(u_ref.dtype)
    o += pool_w
    dt_ref[0] = _dot(h, wdt_ref[...])
    cos, sin = cos_ref[0], sin_ref[0]
    scale = ATT_HEAD_DIM ** -0.5
    pairs = att_w // LANES

    def emit(t, which, p):
        slab = slabs.at[which * pairs + p]
        slab[...] = t
        for bi, (_, dil) in enumerate(ATT_ORDER):
            ref = qkv_refs[3 * bi + which]
            if dil == 1:
                ref[p, 0] = t.astype(BF16)
            else:
                for r in range(dil):
                    ref[p, 0, :, r * LANES:(r + 1) * LANES] = (
                        slab[pl.ds(r, tm // dil, stride=dil), :].astype(BF16))

    for p in range(pairs):
        q = _dot(h, w_ref[:, o + p * LANES:o + (p + 1) * LANES])
        emit(_rotate(q, cos, sin) * scale, 0, p)
        k = _dot(h, w_ref[:, o + att_w + p * LANES:o + att_w + (p + 1) * LANES])
        emit(_rotate(k, cos, sin), 1, p)
        emit(_dot(h, w_ref[:, o + 2 * att_w + p * LANES:o + 2 * att_w + (p + 1) * LANES]), 2, p)


def _in_projection(x, mod, gain, w_main, w_dt, cos, sin, widths, tm):
    b, s, d = x.shape
    inner, conv_ch, pool_w, att_w = widths
    pairs = att_w // LANES
    row = lambda i, j: (i, j, 0)
    view_shapes, view_specs = [], []
    for _, dil in ATT_ORDER:
        assert tm % (dil * 2 * SUBLANES) == 0
        view_shapes += [jax.ShapeDtypeStruct((pairs, b, s // dil, dil * LANES), BF16)] * 3
        view_specs += [pl.BlockSpec((pairs, 1, tm // dil, dil * LANES), lambda i, j: (0, i, j, 0))] * 3
    return pl.pallas_call(
        functools.partial(_inproj_kernel, widths),
        out_shape=(
            jax.ShapeDtypeStruct((b, s, inner), BF16),
            jax.ShapeDtypeStruct((b, s, conv_ch), BF16),
            jax.ShapeDtypeStruct((b, s, LANES), F32),
            jax.ShapeDtypeStruct((b, s, pool_w), BF16),
            *view_shapes,
        ),
        grid=(b, s // tm),
        in_specs=[
            pl.BlockSpec((1, tm, d), row),
            pl.BlockSpec((1, SUBLANES, d), lambda i, j: (i, 0, 0)),
            _resident(gain), _resident(w_main), _resident(w_dt),
            pl.BlockSpec((1, tm, LANES), row),
            pl.BlockSpec((1, tm, LANES), row),
        ],
        out_specs=(
            pl.BlockSpec((1, tm, inner), row),
            pl.BlockSpec((1, tm, conv_ch), row),
            pl.BlockSpec((1, tm, LANES), row),
            pl.BlockSpec((1, tm, pool_w), row),
            *view_specs,
        ),
        scratch_shapes=[pltpu.VMEM((3 * pairs, tm, LANES), F32)],
        compiler_params=_params("arbitrary", "arbitrary"),
        name="in_projection",
    )(x, mod, gain, w_main, w_dt, cos, sin)


def _pair_rep(col, h0):
    lane = lax.broadcasted_iota(jnp.int32, (col.shape[0], LANES), 1)
    return jnp.where(lane < SSD_HEAD_DIM, col[:, h0:h0 + 1], col[:, h0 + 1:h0 + 2])


def _ssd_pool_kernel(dims, z_ref, xbc_ref, dt_ref, u_ref, cw_ref, cb_ref, dtb_ref, alog_ref, dskip_ref,
                     ng_ref, pw_ref, ps_ref, o_ref, xbuf, ubuf, hstate):
    inner, heads = dims
    q = SSD_CHUNK
    gw = inner // SSD_GROUPS
    n = SSD_STATE
    chunk = pl.program_id(1)

    @pl.when(chunk == 0)
    def _():
        xbuf[0:CONV_HALO, :] = jnp.zeros((CONV_HALO, xbuf.shape[1]), F32)
        ubuf[0:POOL_HALO, :] = jnp.zeros((POOL_HALO, ubuf.shape[1]), F32)
        hstate[...] = jnp.zeros(hstate.shape, F32)

    xbuf[CONV_HALO:CONV_HALO + q, :] = xbc_ref[0].astype(F32)
    ext = xbuf[...]
    conv = cb_ref[...] + cw_ref[SSD_CONV_K - 1:SSD_CONV_K, :] * ext
    for k in range(1, SSD_CONV_K):
        conv = conv + cw_ref[SSD_CONV_K - 1 - k:SSD_CONV_K - k, :] * pltpu.roll(ext, k, 0)
    xbc = _silu(conv[CONV_HALO:, :])
    xbuf[0:CONV_HALO, :] = ext[q:q + CONV_HALO, :]

    lane = lax.broadcasted_iota(jnp.int32, (q, LANES), 1)
    dt = jax.nn.softplus(dt_ref[0] + dtb_ref[...])
    a = -jnp.exp(alog_ref[...])
    da = jnp.where(lane < heads, dt * a, 0.0)
    rows = lax.broadcasted_iota(jnp.int32, (q, q), 0)
    cols = lax.broadcasted_iota(jnp.int32, (q, q), 1)
    causal = rows >= cols
    tri = jnp.where(causal, 1.0, 0.0).astype(F32)
    a_col = jnp.dot(tri, da, preferred_element_type=F32, precision=lax.Precision.HIGHEST)
    a_row = a_col.T

    pair_lane = lax.broadcasted_iota(jnp.int32, (q, LANES), 1)
    left = pair_lane < SSD_HEAD_DIM
    hpg = heads // SSD_GROUPS
    y_groups = []
    for g in range(SSD_GROUPS):
        xs = xbc[:, g * gw:(g + 1) * gw]
        bm_f = xbc[:, inner + g * n:inner + (g + 1) * n]
        bm = bm_f.astype(BF16)
        bm_t = bm_f.T.astype(BF16)
        cm = xbc[:, inner + SSD_GROUPS * n + g * n:inner + SSD_GROUPS * n + (g + 1) * n].astype(BF16)
        cb = _dot_nt(cm, bm)
        y_off = _dot(cm, hstate[g].astype(BF16))
        y_pairs, xd_pairs, cd_pairs = [], [], []
        for p in range(hpg // 2):
            h0 = g * hpg + 2 * p
            xs_p = xs[:, p * LANES:(p + 1) * LANES]
            acum = _pair_rep(a_col, h0)
            xdt = xs_p * _pair_rep(dt, h0)
            y_p = y_off[:, p * LANES:(p + 1) * LANES] * jnp.exp(acum)
            for side in range(2):
                hh = h0 + side
                seg = a_col[:, hh:hh + 1] - a_row[hh:hh + 1, :]
                lmat = jnp.exp(jnp.where(causal, seg, -jnp.inf))
                keep = left if side == 0 else jnp.logical_not(left)
                y_p = y_p + _dot((cb * lmat).astype(BF16), jnp.where(keep, xdt, 0.0).astype(BF16))
            a_last = acum[q - 1:q, :]
            xd_pairs.append((xdt * jnp.exp(a_last - acum)).astype(BF16))
            cd_pairs.append(jnp.exp(a_last))
            y_pairs.append(y_p + dskip_ref[:, h0 * SSD_HEAD_DIM:(h0 + 2) * SSD_HEAD_DIM] * xs_p)
        xd = jnp.concatenate(xd_pairs, axis=1)
        hstate[g] = hstate[g] * jnp.concatenate(cd_pairs, axis=1) + _dot(bm_t, xd)
        y = jnp.concatenate(y_pairs, axis=1) * _silu(z_ref[0, :, g * gw:(g + 1) * gw].astype(F32))
        y = y * lax.rsqrt(jnp.mean(y * y, axis=-1, keepdims=True) + NORM_EPS)
        y_groups.append(y * ng_ref[:, g * gw:(g + 1) * gw])
    o_ref[0, :, 0:inner] = jnp.concatenate(y_groups, axis=1).astype(o_ref.dtype)

    pool_w = ubuf.shape[1]
    pool_ch = pool_w // len(POOL_WINDOWS)
    u = u_ref[0].astype(F32)
    ubuf[POOL_HALO:POOL_HALO + q, :] = u
    ext = ubuf[...]
    plane = lax.broadcasted_iota(jnp.int32, (q, pool_w), 1)
    pos = chunk * q + lax.broadcasted_iota(jnp.int32, (q, pool_w), 0)
    pooled = jnp.zeros((q, pool_w), F32)
    win = jnp.zeros((q, pool_w), jnp.int32)
    acc, width = ext, 1
    for gi, w in enumerate(POOL_WINDOWS):
        while width < w:
            acc = acc + pltpu.roll(acc, width, 0)
            width *= 2
        assert width == w
        in_group = (plane >= gi * pool_ch) & (plane < (gi + 1) * pool_ch)
        pooled = jnp.where(in_group, acc[POOL_HALO:, :], pooled)
        win = jnp.where(in_group, w, win)
    cnt = jnp.minimum(pos + 1, win).astype(F32)
    diff = (pooled / cnt - u).astype(BF16)
    ubuf[0:POOL_HALO, :] = ext[q:q + POOL_HALO, :]
    o_ref[0, :, inner:inner + pool_w] = (_dot(diff, pw_ref[...]) * ps_ref[...]).astype(o_ref.dtype)


def _ssd_pool(z, xbc, dt, u, cw, cb, dtb, alog, dskip, ng, pw, ps, heads):
    b, s, inner = z.shape
    conv_ch = xbc.shape[-1]
    pool_w = u.shape[-1]
    q = SSD_CHUNK
    row = lambda i, j: (i, j, 0)
    full = _resident
    out_w = inner + pool_w
    return pl.pallas_call(
        functools.partial(_ssd_pool_kernel, (inner, heads)),
        out_shape=jax.ShapeDtypeStruct((b, s, out_w), BF16),
        grid=(b, s // q),
        in_specs=[
            pl.BlockSpec((1, q, inner), row),
            pl.BlockSpec((1, q, conv_ch), row),
            pl.BlockSpec((1, q, LANES), row),
            pl.BlockSpec((1, q, pool_w), row),
            full(cw), full(cb), full(dtb), full(alog), full(dskip), full(ng), full(pw), full(ps),
        ],
        out_specs=pl.BlockSpec((1, q, out_w), row),
        scratch_shapes=[
            pltpu.VMEM((CONV_HALO + q, conv_ch), F32),
            pltpu.VMEM((POOL_HALO + q, pool_w), F32),
            pltpu.VMEM((SSD_GROUPS, SSD_STATE, inner // SSD_GROUPS), F32),
        ],
        compiler_params=_params("arbitrary", "arbitrary"),
        name="ssd_pool",
    )(z, xbc, dt, u, cw, cb, dtb, alog, dskip, ng, pw, ps)


def _attn_kernel(seq, *refs):
    nbr = len(ATT_ORDER)
    qkv_refs = refs[:3 * nbr]
    o_ref = refs[3 * nbr]
    acc_ref, l_ref, m0_ref, m1_ref = refs[3 * nbr + 1:]
    blk = ATT_BLOCK
    hd = ATT_HEAD_DIM
    lane = lax.broadcasted_iota(jnp.int32, (blk, LANES), 1)
    left = lane < hd
    qi = lax.broadcasted_iota(jnp.int32, (blk, blk), 0)
    kj = lax.broadcasted_iota(jnp.int32, (blk, blk), 1)
    mask_cur = kj <= qi
    mask_prev = kj >= qi
    neg = -jnp.inf

    keeps = (left, jnp.logical_not(left))
    left_band = lax.broadcasted_iota(jnp.int32, (2 * blk, LANES), 1) < hd
    keeps_band = (left_band, jnp.logical_not(left_band))

    def softmax_blocks(qs, kbands, vbands, pens, states):
        nb = len(qs)
        zero = jnp.zeros_like(qs[0])
        one = jnp.ones_like(vbands[0])
        scores = []
        for u in range(nb):
            for side in range(2):
                s = _dot_nt(jnp.where(keeps[side], qs[u], zero), kbands[u])
                s_p = s[:, :blk] if pens[u] is None else s[:, :blk] + pens[u]
                scores.append((jnp.where(mask_prev, s_p, neg), jnp.where(mask_cur, s[:, blk:], neg)))
        m_new = []
        for u in range(nb):
            for side in range(2):
                s_p, s_c = scores[2 * u + side]
                m_blk = jnp.max(jnp.maximum(s_p, s_c), axis=-1, keepdims=True)
                if states[u] is None:
                    m_new.append(jnp.broadcast_to(m_blk, (blk, LANES)))
                else:
                    m_new.append(jnp.maximum(states[u][side], m_blk))
        res = []
        for u in range(nb):
            for side in range(2):
                s_p, s_c = scores[2 * u + side]
                m = m_new[2 * u + side]
                p = jnp.concatenate([jnp.exp(s_p - m), jnp.exp(s_c - m)], axis=1).astype(BF16)
                res.append(_dot(p, jnp.where(keeps_band[side], vbands[u], one)))
        outs = []
        for u in range(nb):
            acc_new = jnp.where(left, res[2 * u], res[2 * u + 1])
            l_new = jnp.where(left, res[2 * u + 1], res[2 * u])
            if states[u] is not None:
                alpha = [jnp.exp(states[u][side] - m_new[2 * u + side]) for side in range(2)]
                acc_new = jnp.where(left, alpha[0], alpha[1]) * states[u][2] + acc_new
                l_new = jnp.where(left, alpha[1], alpha[0]) * states[u][3] + l_new
            outs.append((m_new[2 * u], m_new[2 * u + 1], acc_new, l_new))
        return outs

    for bi, (window, dil) in enumerate(ATT_ORDER):
        assert window // dil == blk
        q_ref, k_ref, v_ref = qkv_refs[3 * bi:3 * bi + 3]
        nblk = seq // dil // blk
        first = bi == 0
        last = bi == nbr - 1
        unroll = min(ATT_UNROLL_MAX, nblk)
        assert nblk % unroll == 0 and (not last or dil == 1)
        for r in range(dil):
            col = r * LANES

            def body(it, carry, first=first, last=last, dil=dil, r=r, col=col, unroll=unroll,
                     q_ref=q_ref, k_ref=k_ref, v_ref=v_ref):
                n0 = it * unroll

                def tile(ref, n):
                    if isinstance(n, int):
                        return ref[0, 0, n * blk:(n + 1) * blk, col:col + LANES]
                    return ref[0, 0, pl.ds(pl.multiple_of(n * blk, blk), blk), col:col + LANES]

                def state_rows(n):
                    if dil == 1:
                        return pl.ds(n * blk if isinstance(n, int) else pl.multiple_of(n * blk, blk), blk)
                    return pl.ds(n * (blk * dil) + r, blk, stride=dil)

                n_prev = max(n0 - 1, 0) if isinstance(n0, int) else jnp.maximum(n0 - 1, 0)
                ks = [tile(k_ref, n_prev)] + [tile(k_ref, n0 + u) for u in range(unroll)]
                vs = [tile(v_ref, n_prev)] + [tile(v_ref, n0 + u) for u in range(unroll)]
                qs = [tile(q_ref, n0 + u) for u in range(unroll)]
                rows = [state_rows(n0 + u) for u in range(unroll)]
                states = [None if first else (m0_ref[rw, :], m1_ref[rw, :], acc_ref[rw, :], l_ref[rw, :])
                          for rw in rows]
                if isinstance(n0, int):
                    pen0 = None if n0 > 0 else neg
                else:
                    pen0 = jnp.where(n0 > 0, 0.0, neg).astype(F32)
                kbands = [jnp.concatenate([ks[u], ks[u + 1]], axis=0) for u in range(unroll)]
                vbands = [jnp.concatenate([vs[u], vs[u + 1]], axis=0) for u in range(unroll)]
                outs = softmax_blocks(qs, kbands, vbands, [pen0] + [None] * (unroll - 1), states)
                for rw, (m0_new, m1_new, acc_new, l_new) in zip(rows, outs):
                    if last:
                        o_ref[0, 0, rw, :] = (acc_new / pltpu.roll(l_new, hd, 1)).astype(o_ref.dtype)
                    else:
                        m0_ref[rw, :] = m0_new
                        m1_ref[rw, :] = m1_new
                        acc_ref[rw, :] = acc_new
                        l_ref[rw, :] = l_new
                return carry

            if nblk == unroll:
                body(0, 0)
            else:
                lax.fori_loop(0, nblk // unroll, body, 0)


def _dilated_attention(views):
    pairs, b = views[-1].shape[:2]
    s = views[-1].shape[2] * ATT_ORDER[-1][1]
    specs = [pl.BlockSpec((1, 1) + t.shape[2:], lambda i, j: (j, i, 0, 0)) for t in views]
    return pl.pallas_call(
        functools.partial(_attn_kernel, s),
        out_shape=jax.ShapeDtypeStruct((pairs, b, s, LANES), BF16),
        grid=(b, pairs),
        in_specs=specs,
        out_specs=pl.BlockSpec((1, 1, s, LANES), lambda i, j: (j, i, 0, 0)),
        scratch_shapes=[pltpu.VMEM((s, LANES), F32)] * 4,
        compiler_params=_params("arbitrary", "arbitrary"),
        name="dilated_attention",
    )(*views)


FFN_COLS = 256
OUT_COLS = 256


def _mix_ffn_kernel(cfg, x_ref, mod_ref, mix_ref, att_ref, wout_ref, g_ref, up_ref, cw_ref, cb_ref, down_ref,
                    fg_ref, o_ref, x1, carry, act):
    ffn, final_norm = cfg
    tm, d = x_ref.shape[1:]
    fc = FFN_COLS
    halo = SUBLANES
    mw = mix_ref.shape[-1]

    for c0 in range(0, d, OUT_COLS):
        y = _dot(mix_ref[0], wout_ref[0:mw, c0:c0 + OUT_COLS])
        for p in range(att_ref.shape[0]):
            y = y + _dot(att_ref[p, 0], wout_ref[mw + p * LANES:mw + (p + 1) * LANES, c0:c0 + OUT_COLS])
        x1[:, c0:c0 + OUT_COLS] = x_ref[0, :, c0:c0 + OUT_COLS] + mod_ref[0, 2:3, c0:c0 + OUT_COLS] * y

    h = _modulated_norm(x1[...], g_ref[...], mod_ref[0, 3:4, :], mod_ref[0, 4:5, :]).astype(BF16)

    @pl.when(pl.program_id(1) == 0)
    def _():
        carry[...] = jnp.zeros(carry.shape, F32)

    top_row = lax.broadcasted_iota(jnp.int32, (halo, fc), 0)

    def conv_act(c0):
        hid = _dot(h, up_ref[:, c0:c0 + fc])
        prev = carry[:, c0:c0 + fc]
        carry[:, c0:c0 + fc] = hid[tm - halo:tm, :]
        out = cb_ref[:, c0:c0 + fc] + cw_ref[FFN_CONV_K - 1:FFN_CONV_K, c0:c0 + fc] * hid
        for k in range(1, FFN_CONV_K):
            sh = pltpu.roll(hid, k, 0)
            top = jnp.where(top_row < k, pltpu.roll(prev, k, 0), sh[0:halo, :])
            sh = jnp.concatenate([top, sh[halo:, :]], axis=0)
            out = out + cw_ref[FFN_CONV_K - 1 - k:FFN_CONV_K - k, c0:c0 + fc] * sh
        return out

    for j in range(ffn // fc):
        gate = conv_act(j * fc)
        val = conv_act(ffn + j * fc)
        act[:, j * fc:(j + 1) * fc] = (_silu(gate) * val).astype(BF16)

    for c0 in range(0, d, OUT_COLS):
        y = _dot(act[...], down_ref[:, c0:c0 + OUT_COLS])
        o_ref[0, :, c0:c0 + OUT_COLS] = x1[:, c0:c0 + OUT_COLS] + mod_ref[0, 5:6, c0:c0 + OUT_COLS] * y
    if final_norm:
        y = o_ref[0]
        o_ref[0] = y * lax.rsqrt(jnp.mean(y * y, axis=-1, keepdims=True) + NORM_EPS) * fg_ref[...]


def _mix_ffn(x, mod, mix, att, w_out, gain, up, cw, cb, down, final_g, final_norm, tm):
    b, s, d = x.shape
    ffn = down.shape[0]
    pairs = att.shape[0]
    assert ffn % FFN_COLS == 0 and d % OUT_COLS == 0
    row = lambda i, j: (i, j, 0)
    return pl.pallas_call(
        functools.partial(_mix_ffn_kernel, (ffn, final_norm)),
        out_shape=jax.ShapeDtypeStruct((b, s, d), F32),
        grid=(b, s // tm),
        in_specs=[
            pl.BlockSpec((1, tm, d), row),
            pl.BlockSpec((1, SUBLANES, d), lambda i, j: (i, 0, 0)),
            pl.BlockSpec((1, tm, mix.shape[-1]), row),
            pl.BlockSpec((pairs, 1, tm, LANES), lambda i, j: (0, i, j, 0)),
            _resident(w_out), _resident(gain), _resident(up), _resident(cw), _resident(cb), _resident(down),
            _resident(final_g),
        ],
        out_specs=pl.BlockSpec((1, tm, d), row),
        scratch_shapes=[
            pltpu.VMEM((tm, d), F32),
            pltpu.VMEM((SUBLANES, 2 * ffn), F32),
            pltpu.VMEM((tm, ffn), BF16),
        ],
        compiler_params=_params("arbitrary", "arbitrary"),
        name="mix_ffn",
    )(x, mod, mix, att, w_out, gain, up, cw, cb, down, final_g)


def _pad_lanes(v, width=LANES):
    return jnp.pad(v, ((0, 0), (0, width - v.shape[-1])))


def kernel(x, c, positions, ada_w, ada_b, norm1_g, w_in, ssd_conv_w, ssd_conv_b, ssd_dt_bias, ssd_a_log,
           ssd_d, ssd_norm_g, pool_w, pool_scale, w_out, norm2_g, ffn_up, ffn_conv_w, ffn_conv_b, ffn_down,
           final_g):
    depth = w_in.shape[0]
    b, s, d = x.shape
    heads = ssd_dt_bias.shape[-1]
    inner = heads * SSD_HEAD_DIM
    conv_ch = ssd_conv_w.shape[-1]
    pool_width = pool_scale.shape[-1]
    ssd_proj = inner + conv_ch + heads
    att_w = (w_in.shape[-1] - ssd_proj - pool_width) // 3
    widths = (inner, conv_ch, pool_width, att_w)
    assert conv_ch == inner + 2 * SSD_GROUPS * SSD_STATE and att_w % LANES == 0
    assert s % (ATT_BLOCK * ATT_PATTERNS[-1][1]) == 0 and s % SSD_CHUNK == 0
    tm = 512

    mod_all = _ada_mod(c, ada_w, ada_b).reshape(depth, b, 6, d)
    mod_all = jnp.pad(mod_all, ((0, 0), (0, 0), (0, SUBLANES - 6), (0, 0)))
    cos, sin = _rope_tables(positions)
    a0 = ssd_proj

    for i in range(depth):
        mod = mod_all[i]
        wi = w_in[i]
        w_main = jnp.concatenate([wi[:, :inner + conv_ch], wi[:, a0:]], axis=1).astype(BF16)
        w_dt = _pad_lanes(wi[:, inner + conv_ch:a0]).astype(BF16)
        z, xbc, dt, u, *views = _in_projection(x, mod, norm1_g[i][None], w_main, w_dt, cos, sin, widths, tm)
        pw_bd = jax.scipy.linalg.block_diag(*[pool_w[i, g] for g in range(pool_w.shape[1])]).astype(BF16)
        mix = _ssd_pool(
            z, xbc, dt, u, ssd_conv_w[i], ssd_conv_b[i][None], _pad_lanes(ssd_dt_bias[i][None]),
            _pad_lanes(ssd_a_log[i][None]), jnp.repeat(ssd_d[i], SSD_HEAD_DIM)[None], ssd_norm_g[i][None],
            pw_bd, pool_scale[i][None], heads)
        att = _dilated_attention(views)
        x = _mix_ffn(x, mod, mix, att, w_out[i].astype(BF16), norm2_g[i][None], ffn_up[i].astype(BF16),
                     ffn_conv_w[i], ffn_conv_b[i][None], ffn_down[i].astype(BF16), final_g[None],
                     i == depth - 1, tm)
    return x
```

```python
import functools

import jax
import jax.numpy as jnp
from jax import lax
from jax.experimental import pallas as pl
from jax.experimental.pallas import tpu as pltpu

F32 = jnp.float32
BF16 = jnp.bfloat16

LANES = 128
SUBLANES = 8
VMEM_LIMIT = 56 * 1024 * 1024

NORM_EPS = 1e-6
ROPE_THETA = 500000.0
LOG2_E = 1.4426950408889634

SSD_HEAD_DIM = 64
SSD_GROUPS = 2
SSD_STATE = 128
SSD_CONV_K = 4
SSD_CHUNK = 256
POOL_WINDOWS = (2, 4, 8, 16)
ATT_HEAD_DIM = 64
ATT_PATTERNS = ((128, 1), (512, 4), (2048, 16))
ATT_BLOCK = 128
ATT_ORDER = tuple(sorted(ATT_PATTERNS, key=lambda wd: -wd[1]))
ATT_UNROLL_MAX = 4
ROT_DIM = ATT_HEAD_DIM // 4
FFN_CONV_K = 3

CONV_HALO = SUBLANES
POOL_HALO = 2 * SUBLANES


def _params(*semantics):
    return pltpu.CompilerParams(dimension_semantics=semantics, vmem_limit_bytes=VMEM_LIMIT)


def _resident(a):
    return pl.BlockSpec(a.shape, lambda *_: (0,) * a.ndim, pipeline_mode=pl.Buffered(1))


def _silu(v):
    return v * jax.nn.sigmoid(v)


def _dot(a, b):
    return jnp.dot(a, b, preferred_element_type=F32)


def _dot_nt(a, b):
    return lax.dot_general(a, b, (((1,), (1,)), ((), ())), preferred_element_type=F32)


def _ada_kernel(c_ref, w_ref, b_ref, o_ref):
    c_act = _silu(c_ref[...])
    o_ref[0] = jnp.dot(c_act, w_ref[0], preferred_element_type=F32,
                       precision=lax.Precision.HIGHEST) + b_ref[0]


def _ada_mod(c, ada_w, ada_b):
    depth, d, n = ada_w.shape
    b = c.shape[0]
    tn = d
    return pl.pallas_call(
        _ada_kernel,
        out_shape=jax.ShapeDtypeStruct((depth, b, n), F32),
        grid=(depth, n // tn),
        in_specs=[
            pl.BlockSpec((b, d), lambda i, j: (0, 0)),
            pl.BlockSpec((1, d, tn), lambda i, j: (i, 0, j)),
            pl.BlockSpec((1, 1, tn), lambda i, j: (i, 0, j)),
        ],
        out_specs=pl.BlockSpec((1, b, tn), lambda i, j: (i, 0, j)),
        compiler_params=_params("arbitrary", "arbitrary"),
        name="ada_mod",
    )(c, ada_w, ada_b.reshape(depth, 1, n))


def _rope_kernel(pos_ref, freq_ref, ecos_ref, esin_ref, cos_ref, sin_ref):
    ang = freq_ref[...] * pos_ref[0]
    rows = [jnp.cos(ang), jnp.sin(ang), jnp.ones_like(ang)]
    rows.append(jnp.zeros((LANES - len(rows) * ang.shape[0], ang.shape[1]), F32))
    t = jnp.concatenate(rows, axis=0).T
    cos_ref[0] = jnp.dot(t, ecos_ref[...], preferred_element_type=F32, precision=lax.Precision.HIGHEST)
    sin_ref[0] = jnp.dot(t, esin_ref[...], preferred_element_type=F32, precision=lax.Precision.HIGHEST)


def _rope_tables(positions):
    b, s = positions.shape
    half = ROT_DIM // 2
    assert half == SUBLANES
    inv_freq = ROPE_THETA ** (-jnp.arange(0, ROT_DIM, 2, dtype=F32) / ROT_DIM)
    lane = jnp.arange(LANES)[None, :] % ATT_HEAD_DIM
    row = jnp.arange(LANES)[:, None]
    rotary = lane < ROT_DIM
    ecos = jnp.where(rotary, row == lane % half, row == 2 * half).astype(F32)
    sign = jnp.where(lane < half, -1.0, 1.0)
    esin = jnp.where(rotary & (row == half + lane % half), sign, 0.0).astype(F32)
    return pl.pallas_call(
        _rope_kernel,
        out_shape=(jax.ShapeDtypeStruct((b, s, LANES), F32),) * 2,
        grid=(b,),
        in_specs=[
            pl.BlockSpec((1, 1, s), lambda i: (i, 0, 0)),
            pl.BlockSpec((half, 1), lambda i: (0, 0)),
            pl.BlockSpec((LANES, LANES), lambda i: (0, 0)),
            pl.BlockSpec((LANES, LANES), lambda i: (0, 0)),
        ],
        out_specs=(pl.BlockSpec((1, s, LANES), lambda i: (i, 0, 0)),) * 2,
        compiler_params=_params("arbitrary"),
        name="rope_tables",
    )(positions.astype(F32)[:, None, :], inv_freq[:, None], ecos, esin)


def _modulated_norm(x, gain, shift, scale):
    y = x * lax.rsqrt(jnp.mean(x * x, axis=-1, keepdims=True) + NORM_EPS)
    return (y * gain) * (1.0 + scale) + shift


def _rotate(t, cos, sin):
    half = ROT_DIM // 2
    lane = lax.broadcasted_iota(jnp.int32, t.shape, 1) % ATT_HEAD_DIM
    partner = jnp.where(lane < half, pltpu.roll(t, LANES - half, 1), pltpu.roll(t, half, 1))
    return t * cos + partner * sin


def _inproj_kernel(widths, x_ref, mod_ref, g_ref, w_ref, wdt_ref, cos_ref, sin_ref,
                   z_ref, xbc_ref, dt_ref, u_ref, *rest):
    qkv_refs, slabs = rest[:-1], rest[-1]
    inner, conv_ch, pool_w, att_w = widths
    tm = x_ref.shape[1]
    h = _modulated_norm(x_ref[0], g_ref[...], mod_ref[0, 0:1, :], mod_ref[0, 1:2, :]).astype(BF16)
    o = 0
    z_ref[0] = _dot(h, w_ref[:, o:o + inner]).astype(z_ref.dtype)
    o += inner
    xbc_ref[0] = _dot(h, w_ref[:, o:o + conv_ch]).astype(xbc_ref.dtype)
    o += conv_ch
    u_ref[0] = _dot(h, w_ref[:, o:o + pool_w]).astype(u_ref.dtype)
    o += pool_w
    dt_ref[0] = _dot(h, wdt_ref[...])
    cos, sin = cos_ref[0], sin_ref[0]
    scale = ATT_HEAD_DIM ** -0.5 * LOG2_E
    pairs = att_w // LANES

    def emit(t, which, p):
        slab = slabs.at[which * pairs + p]
        slab[...] = t
        for bi, (_, dil) in enumerate(ATT_ORDER):
            ref = qkv_refs[3 * bi + which]
            if dil == 1:
                ref[p, 0] = t.astype(BF16)
            else:
                for r in range(dil):
                    ref[p, 0, :, r * LANES:(r + 1) * LANES] = (
                        slab[pl.ds(r, tm // dil, stride=dil), :].astype(BF16))

    for p in range(pairs):
        qk = _dot(h, w_ref[:, o + 2 * p * LANES:o + 2 * (p + 1) * LANES])
        emit(_rotate(qk[:, :LANES], cos, sin) * scale, 0, p)
        emit(_rotate(qk[:, LANES:], cos, sin), 1, p)
    v = _dot(h, w_ref[:, o + 2 * att_w:o + 3 * att_w])
    for p in range(pairs):
        emit(v[:, p * LANES:(p + 1) * LANES], 2, p)


def _in_projection(x, mod, gain, w_main, w_dt, cos, sin, widths, tm):
    b, s, d = x.shape
    inner, conv_ch, pool_w, att_w = widths
    pairs = att_w // LANES
    row = lambda i, j: (i, j, 0)
    view_shapes, view_specs = [], []
    for _, dil in ATT_ORDER:
        assert tm % (dil * 2 * SUBLANES) == 0
        view_shapes += [jax.ShapeDtypeStruct((pairs, b, s // dil, dil * LANES), BF16)] * 3
        view_specs += [pl.BlockSpec((pairs, 1, tm // dil, dil * LANES), lambda i, j: (0, i, j, 0))] * 3
    return pl.pallas_call(
        functools.partial(_inproj_kernel, widths),
        out_shape=(
            jax.ShapeDtypeStruct((b, s, inner), BF16),
            jax.ShapeDtypeStruct((b, s, conv_ch), BF16),
            jax.ShapeDtypeStruct((b, s, LANES), F32),
            jax.ShapeDtypeStruct((b, s, pool_w), BF16),
            *view_shapes,
        ),
        grid=(b, s // tm),
        in_specs=[
            pl.BlockSpec((1, tm, d), row),
            pl.BlockSpec((1, SUBLANES, d), lambda i, j: (i, 0, 0)),
            _resident(gain), _resident(w_main), _resident(w_dt),
            pl.BlockSpec((1, tm, LANES), row),
            pl.BlockSpec((1, tm, LANES), row),
        ],
        out_specs=(
            pl.BlockSpec((1, tm, inner), row),
            pl.BlockSpec((1, tm, conv_ch), row),
            pl.BlockSpec((1, tm, LANES), row),
            pl.BlockSpec((1, tm, pool_w), row),
            *view_specs,
        ),
        scratch_shapes=[pltpu.VMEM((3 * pairs, tm, LANES), F32)],
        compiler_params=_params("arbitrary", "arbitrary"),
        name="in_projection",
    )(x, mod, gain, w_main, w_dt, cos, sin)


def _pair_rep(col, h0):
    lane = lax.broadcasted_iota(jnp.int32, (col.shape[0], LANES), 1)
    return jnp.where(lane < SSD_HEAD_DIM, col[:, h0:h0 + 1], col[:, h0 + 1:h0 + 2])


def _ssd_pool_kernel(dims, z_ref, xbc_ref, dt_ref, u_ref, cw_ref, cb_ref, dtb_ref, alog_ref, dskip_ref,
                     ng_ref, pw_ref, ps_ref, o_ref, xbuf, ubuf, hstate):
    inner, heads = dims
    q = SSD_CHUNK
    blk_rows = z_ref.shape[1]
    gw = inner // SSD_GROUPS
    n = SSD_STATE
    step = pl.program_id(1)

    @pl.when(step == 0)
    def _():
        xbuf[0:CONV_HALO, :] = jnp.zeros((CONV_HALO, xbuf.shape[1]), F32)
        ubuf[0:POOL_HALO, :] = jnp.zeros((POOL_HALO, ubuf.shape[1]), F32)
        hstate[...] = jnp.zeros(hstate.shape, F32)

    xbuf[CONV_HALO:CONV_HALO + blk_rows, :] = xbc_ref[0].astype(F32)
    ext = xbuf[...]
    conv = cb_ref[...] + cw_ref[SSD_CONV_K - 1:SSD_CONV_K, :] * ext
    for k in range(1, SSD_CONV_K):
        conv = conv + cw_ref[SSD_CONV_K - 1 - k:SSD_CONV_K - k, :] * pltpu.roll(ext, k, 0)
    xbc_all = _silu(conv[CONV_HALO:, :])
    xbuf[0:CONV_HALO, :] = ext[blk_rows:blk_rows + CONV_HALO, :]

    lane = lax.broadcasted_iota(jnp.int32, (blk_rows, LANES), 1)
    dt_all = jax.nn.softplus(dt_ref[0] + dtb_ref[...])
    a = -jnp.exp(alog_ref[...])
    da_all = jnp.where(lane < heads, dt_all * a, 0.0)
    rows = lax.broadcasted_iota(jnp.int32, (q, q), 0)
    cols = lax.broadcasted_iota(jnp.int32, (q, q), 1)
    causal = rows >= cols
    tri = jnp.where(causal, 1.0, 0.0).astype(F32)
    left = lax.broadcasted_iota(jnp.int32, (q, LANES), 1) < SSD_HEAD_DIM
    hpg = heads // SSD_GROUPS

    for ci in range(blk_rows // q):
        r0 = ci * q
        xbc = xbc_all[r0:r0 + q, :]
        dt = dt_all[r0:r0 + q, :]
        a_col = jnp.dot(tri, da_all[r0:r0 + q, :], preferred_element_type=F32,
                        precision=lax.Precision.HIGHEST)
        a_row = a_col.T
        y_groups = []
        for g in range(SSD_GROUPS):
            xs = xbc[:, g * gw:(g + 1) * gw]
            bm_f = xbc[:, inner + g * n:inner + (g + 1) * n]
            bm = bm_f.astype(BF16)
            bm_t = bm_f.T.astype(BF16)
            cm = xbc[:, inner + SSD_GROUPS * n + g * n:inner + SSD_GROUPS * n + (g + 1) * n].astype(BF16)
            cb = _dot_nt(cm, bm)
            y_off = _dot(cm, hstate[g].astype(BF16))
            y_pairs, xd_pairs, cd_pairs = [], [], []
            for p in range(hpg // 2):
                h0 = g * hpg + 2 * p
                xs_p = xs[:, p * LANES:(p + 1) * LANES]
                acum = _pair_rep(a_col, h0)
                xdt = xs_p * _pair_rep(dt, h0)
                y_p = y_off[:, p * LANES:(p + 1) * LANES] * jnp.exp(acum)
                for side in range(2):
                    hh = h0 + side
                    seg = a_col[:, hh:hh + 1] - a_row[hh:hh + 1, :]
                    lmat = jnp.exp(jnp.where(causal, seg, -jnp.inf))
                    keep = left if side == 0 else jnp.logical_not(left)
                    y_p = y_p + _dot((cb * lmat).astype(BF16), jnp.where(keep, xdt, 0.0).astype(BF16))
                a_last = acum[q - 1:q, :]
                xd_pairs.append((xdt * jnp.exp(a_last - acum)).astype(BF16))
                cd_pairs.append(jnp.exp(a_last))
                y_pairs.append(y_p + dskip_ref[:, h0 * SSD_HEAD_DIM:(h0 + 2) * SSD_HEAD_DIM] * xs_p)
            xd = jnp.concatenate(xd_pairs, axis=1)
            hstate[g] = hstate[g] * jnp.concatenate(cd_pairs, axis=1) + _dot(bm_t, xd)
            y = jnp.concatenate(y_pairs, axis=1) * _silu(z_ref[0, r0:r0 + q, g * gw:(g + 1) * gw].astype(F32))
            y = y * lax.rsqrt(jnp.mean(y * y, axis=-1, keepdims=True) + NORM_EPS)
            y_groups.append(y * ng_ref[:, g * gw:(g + 1) * gw])
        o_ref[0, r0:r0 + q, 0:inner] = jnp.concatenate(y_groups, axis=1).astype(o_ref.dtype)

    pool_w = ubuf.shape[1]
    pool_ch = pool_w // len(POOL_WINDOWS)
    u = u_ref[0].astype(F32)
    ubuf[POOL_HALO:POOL_HALO + blk_rows, :] = u
    ext = ubuf[...]
    plane = lax.broadcasted_iota(jnp.int32, (blk_rows, pool_w), 1)
    pos = step * blk_rows + lax.broadcasted_iota(jnp.int32, (blk_rows, pool_w), 0)
    pooled = jnp.zeros((blk_rows, pool_w), F32)
    win = jnp.zeros((blk_rows, pool_w), jnp.int32)
    acc, width = ext, 1
    for gi, w in enumerate(POOL_WINDOWS):
        while width < w:
            acc = acc + pltpu.roll(acc, width, 0)
            width *= 2
        assert width == w
        in_group = (plane >= gi * pool_ch) & (plane < (gi + 1) * pool_ch)
        pooled = jnp.where(in_group, acc[POOL_HALO:, :], pooled)
        win = jnp.where(in_group, w, win)
    cnt = jnp.minimum(pos + 1, win).astype(F32)
    diff = (pooled / cnt - u).astype(BF16)
    ubuf[0:POOL_HALO, :] = ext[blk_rows:blk_rows + POOL_HALO, :]
    o_ref[0, :, inner:inner + pool_w] = (_dot(diff, pw_ref[...]) * ps_ref[...]).astype(o_ref.dtype)


def _ssd_pool(z, xbc, dt, u, cw, cb, dtb, alog, dskip, ng, pw, ps, heads, blk_rows):
    b, s, inner = z.shape
    conv_ch = xbc.shape[-1]
    pool_w = u.shape[-1]
    assert blk_rows % SSD_CHUNK == 0 and s % blk_rows == 0
    row = lambda i, j: (i, j, 0)
    full = _resident
    out_w = inner + pool_w
    return pl.pallas_call(
        functools.partial(_ssd_pool_kernel, (inner, heads)),
        out_shape=jax.ShapeDtypeStruct((b, s, out_w), BF16),
        grid=(b, s // blk_rows),
        in_specs=[
            pl.BlockSpec((1, blk_rows, inner), row),
            pl.BlockSpec((1, blk_rows, conv_ch), row),
            pl.BlockSpec((1, blk_rows, LANES), row),
            pl.BlockSpec((1, blk_rows, pool_w), row),
            full(cw), full(cb), full(dtb), full(alog), full(dskip), full(ng), full(pw), full(ps),
        ],
        out_specs=pl.BlockSpec((1, blk_rows, out_w), row),
        scratch_shapes=[
            pltpu.VMEM((CONV_HALO + blk_rows, conv_ch), F32),
            pltpu.VMEM((POOL_HALO + blk_rows, pool_w), F32),
            pltpu.VMEM((SSD_GROUPS, SSD_STATE, inner // SSD_GROUPS), F32),
        ],
        compiler_params=_params("arbitrary", "arbitrary"),
        name="ssd_pool",
    )(z, xbc, dt, u, cw, cb, dtb, alog, dskip, ng, pw, ps)


def _attn_kernel(seq, *refs):
    nbr = len(ATT_ORDER)
    qkv_refs = refs[:3 * nbr]
    o_ref = refs[3 * nbr]
    acc_ref, l_ref, m0_ref, m1_ref = refs[3 * nbr + 1:]
    blk = ATT_BLOCK
    hd = ATT_HEAD_DIM
    lane = lax.broadcasted_iota(jnp.int32, (blk, LANES), 1)
    left = lane < hd
    qi = lax.broadcasted_iota(jnp.int32, (blk, blk), 0)
    kj = lax.broadcasted_iota(jnp.int32, (blk, blk), 1)
    mask_cur = kj <= qi
    mask_prev = kj >= qi
    neg = -jnp.inf

    keeps = (left, jnp.logical_not(left))
    left_band = lax.broadcasted_iota(jnp.int32, (2 * blk, LANES), 1) < hd
    keeps_band = (left_band, jnp.logical_not(left_band))

    def softmax_blocks(qs, kbands, vbands, pens, states):
        nb = len(qs)
        zero = jnp.zeros_like(qs[0])
        one = jnp.ones_like(vbands[0])
        scores = []
        for u in range(nb):
            for side in range(2):
                s = _dot_nt(jnp.where(keeps[side], qs[u], zero), kbands[u])
                s_p = s[:, :blk] if pens[u] is None else s[:, :blk] + pens[u]
                scores.append((jnp.where(mask_prev, s_p, neg), jnp.where(mask_cur, s[:, blk:], neg)))
        m_new = []
        for u in range(nb):
            for side in range(2):
                s_p, s_c = scores[2 * u + side]
                m_blk = jnp.max(jnp.maximum(s_p, s_c), axis=-1, keepdims=True)
                if states[u] is None:
                    m_new.append(jnp.broadcast_to(m_blk, (blk, LANES)))
                else:
                    m_new.append(jnp.maximum(states[u][side], m_blk))
        res = []
        for u in range(nb):
            for side in range(2):
                s_p, s_c = scores[2 * u + side]
                m = m_new[2 * u + side]
                p = jnp.concatenate([jnp.exp2(s_p - m), jnp.exp2(s_c - m)], axis=1).astype(BF16)
                res.append(_dot(p, jnp.where(keeps_band[side], vbands[u], one)))
        outs = []
        for u in range(nb):
            acc_new = jnp.where(left, res[2 * u], res[2 * u + 1])
            l_new = jnp.where(left, res[2 * u + 1], res[2 * u])
            if states[u] is not None:
                alpha = [jnp.exp2(states[u][side] - m_new[2 * u + side]) for side in range(2)]
                acc_new = jnp.where(left, alpha[0], alpha[1]) * states[u][2] + acc_new
                l_new = jnp.where(left, alpha[1], alpha[0]) * states[u][3] + l_new
            outs.append((m_new[2 * u], m_new[2 * u + 1], acc_new, l_new))
        return outs

    for bi, (window, dil) in enumerate(ATT_ORDER):
        assert window // dil == blk
        q_ref, k_ref, v_ref = qkv_refs[3 * bi:3 * bi + 3]
        nblk = seq // dil // blk
        first = bi == 0
        last = bi == nbr - 1
        unroll = min(ATT_UNROLL_MAX, nblk)
        assert nblk % unroll == 0 and (not last or dil == 1)
        for r in range(dil):
            col = r * LANES

            def body(it, carry, first=first, last=last, dil=dil, r=r, col=col, unroll=unroll,
                     q_ref=q_ref, k_ref=k_ref, v_ref=v_ref):
                n0 = it * unroll

                def tile(ref, n):
                    if isinstance(n, int):
                        return ref[0, 0, n * blk:(n + 1) * blk, col:col + LANES]
                    return ref[0, 0, pl.ds(pl.multiple_of(n * blk, blk), blk), col:col + LANES]

                def state_rows(n):
                    if dil == 1:
                        return pl.ds(n * blk if isinstance(n, int) else pl.multiple_of(n * blk, blk), blk)
                    return pl.ds(n * (blk * dil) + r, blk, stride=dil)

                n_prev = max(n0 - 1, 0) if isinstance(n0, int) else jnp.maximum(n0 - 1, 0)
                ks = [tile(k_ref, n_prev)] + [tile(k_ref, n0 + u) for u in range(unroll)]
                vs = [tile(v_ref, n_prev)] + [tile(v_ref, n0 + u) for u in range(unroll)]
                qs = [tile(q_ref, n0 + u) for u in range(unroll)]
                rows = [state_rows(n0 + u) for u in range(unroll)]
                states = [None if first else (m0_ref[rw, :], m1_ref[rw, :], acc_ref[rw, :], l_ref[rw, :])
                          for rw in rows]
                if isinstance(n0, int):
                    pen0 = None if n0 > 0 else neg
                else:
                    pen0 = jnp.where(n0 > 0, 0.0, neg).astype(F32)
                kbands = [jnp.concatenate([ks[u], ks[u + 1]], axis=0) for u in range(unroll)]
                vbands = [jnp.concatenate([vs[u], vs[u + 1]], axis=0) for u in range(unroll)]
                outs = softmax_blocks(qs, kbands, vbands, [pen0] + [None] * (unroll - 1), states)
                for rw, (m0_new, m1_new, acc_new, l_new) in zip(rows, outs):
                    if last:
                        o_ref[0, 0, rw, :] = (acc_new / pltpu.roll(l_new, hd, 1)).astype(o_ref.dtype)
                    else:
                        m0_ref[rw, :] = m0_new
                        m1_ref[rw, :] = m1_new
                        acc_ref[rw, :] = acc_new
                        l_ref[rw, :] = l_new
                return carry

            if nblk == unroll:
                body(0, 0)
            else:
                lax.fori_loop(0, nblk // unroll, body, 0)


def _dilated_attention(views):
    pairs, b = views[-1].shape[:2]
    s = views[-1].shape[2] * ATT_ORDER[-1][1]
    specs = [pl.BlockSpec((1, 1) + t.shape[2:], lambda i, j: (j, i, 0, 0)) for t in views]
    return pl.pallas_call(
        functools.partial(_attn_kernel, s),
        out_shape=jax.ShapeDtypeStruct((pairs, b, s, LANES), BF16),
        grid=(b, pairs),
        in_specs=specs,
        out_specs=pl.BlockSpec((1, 1, s, LANES), lambda i, j: (j, i, 0, 0)),
        scratch_shapes=[pltpu.VMEM((s, LANES), F32)] * 4,
        compiler_params=_params("arbitrary", "arbitrary"),
        name="dilated_attention",
    )(*views)


FFN_COLS = 256
OUT_COLS = 256


def _mix_ffn_kernel(cfg, x_ref, mod_ref, mix_ref, att_ref, wout_ref, g_ref, up_ref, cw_ref, cb_ref, down_ref,
                    fg_ref, o_ref, x1, carry, act):
    ffn, final_norm = cfg
    tm, d = x_ref.shape[1:]
    fc = FFN_COLS
    halo = SUBLANES
    mw = mix_ref.shape[-1]

    for c0 in range(0, d, OUT_COLS):
        y = _dot(mix_ref[0], wout_ref[0:mw, c0:c0 + OUT_COLS])
        for p in range(att_ref.shape[0]):
            y = y + _dot(att_ref[p, 0], wout_ref[mw + p * LANES:mw + (p + 1) * LANES, c0:c0 + OUT_COLS])
        x1[:, c0:c0 + OUT_COLS] = x_ref[0, :, c0:c0 + OUT_COLS] + mod_ref[0, 2:3, c0:c0 + OUT_COLS] * y

    h = _modulated_norm(x1[...], g_ref[...], mod_ref[0, 3:4, :], mod_ref[0, 4:5, :]).astype(BF16)

    @pl.when(pl.program_id(1) == 0)
    def _():
        carry[...] = jnp.zeros(carry.shape, F32)

    top_row = lax.broadcasted_iota(jnp.int32, (halo, fc), 0)

    def conv_act(c0):
        hid = _dot(h, up_ref[:, c0:c0 + fc])
        prev = carry[:, c0:c0 + fc]
        carry[:, c0:c0 + fc] = hid[tm - halo:tm, :]
        out = cb_ref[:, c0:c0 + fc] + cw_ref[FFN_CONV_K - 1:FFN_CONV_K, c0:c0 + fc] * hid
        for k in range(1, FFN_CONV_K):
            sh = pltpu.roll(hid, k, 0)
            top = jnp.where(top_row < k, pltpu.roll(prev, k, 0), sh[0:halo, :])
            sh = jnp.concatenate([top, sh[halo:, :]], axis=0)
            out = out + cw_ref[FFN_CONV_K - 1 - k:FFN_CONV_K - k, c0:c0 + fc] * sh
        return out

    for j in range(ffn // fc):
        gate = conv_act(j * fc)
        val = conv_act(ffn + j * fc)
        act[:, j * fc:(j + 1) * fc] = (_silu(gate) * val).astype(BF16)

    for c0 in range(0, d, OUT_COLS):
        y = _dot(act[...], down_ref[:, c0:c0 + OUT_COLS])
        o_ref[0, :, c0:c0 + OUT_COLS] = x1[:, c0:c0 + OUT_COLS] + mod_ref[0, 5:6, c0:c0 + OUT_COLS] * y
    if final_norm:
        y = o_ref[0]
        o_ref[0] = y * lax.rsqrt(jnp.mean(y * y, axis=-1, keepdims=True) + NORM_EPS) * fg_ref[...]


def _mix_ffn(x, mod, mix, att, w_out, gain, up, cw, cb, down, final_g, final_norm, tm):
    b, s, d = x.shape
    ffn = down.shape[0]
    pairs = att.shape[0]
    assert ffn % FFN_COLS == 0 and d % OUT_COLS == 0
    row = lambda i, j: (i, j, 0)
    return pl.pallas_call(
        functools.partial(_mix_ffn_kernel, (ffn, final_norm)),
        out_shape=jax.ShapeDtypeStruct((b, s, d), F32),
        grid=(b, s // tm),
        in_specs=[
            pl.BlockSpec((1, tm, d), row),
            pl.BlockSpec((1, SUBLANES, d), lambda i, j: (i, 0, 0)),
            pl.BlockSpec((1, tm, mix.shape[-1]), row),
            pl.BlockSpec((pairs, 1, tm, LANES), lambda i, j: (0, i, j, 0)),
            _resident(w_out), _resident(gain), _resident(up), _resident(cw), _resident(cb), _resident(down),
            _resident(final_g),
        ],
        out_specs=pl.BlockSpec((1, tm, d), row),
        scratch_shapes=[
            pltpu.VMEM((tm, d), F32),
            pltpu.VMEM((SUBLANES, 2 * ffn), F32),
            pltpu.VMEM((tm, ffn), BF16),
        ],
        compiler_params=_params("arbitrary", "arbitrary"),
        name="mix_ffn",
    )(x, mod, mix, att, w_out, gain, up, cw, cb, down, final_g)


def _tile_rows(s):
    pick = lambda top: max(t for t in (256, 512, 1024) if t <= top and s % t == 0)
    return pick(1024), pick(512), pick(1024)


def _pad_lanes(v, width=LANES):
    return jnp.pad(v, ((0, 0), (0, width - v.shape[-1])))


def kernel(x, c, positions, ada_w, ada_b, norm1_g, w_in, ssd_conv_w, ssd_conv_b, ssd_dt_bias, ssd_a_log,
           ssd_d, ssd_norm_g, pool_w, pool_scale, w_out, norm2_g, ffn_up, ffn_conv_w, ffn_conv_b, ffn_down,
           final_g):
    depth = w_in.shape[0]
    b, s, d = x.shape
    heads = ssd_dt_bias.shape[-1]
    inner = heads * SSD_HEAD_DIM
    conv_ch = ssd_conv_w.shape[-1]
    pool_width = pool_scale.shape[-1]
    ssd_proj = inner + conv_ch + heads
    att_w = (w_in.shape[-1] - ssd_proj - pool_width) // 3
    widths = (inner, conv_ch, pool_width, att_w)
    assert conv_ch == inner + 2 * SSD_GROUPS * SSD_STATE and att_w % LANES == 0
    assert s % (ATT_BLOCK * ATT_PATTERNS[-1][1]) == 0 and s % SSD_CHUNK == 0
    tm_in, tm_ffn, ssd_rows = _tile_rows(s)

    mod_all = _ada_mod(c, ada_w, ada_b).reshape(depth, b, 6, d)
    mod_all = jnp.pad(mod_all, ((0, 0), (0, 0), (0, SUBLANES - 6), (0, 0)))
    cos, sin = _rope_tables(positions)
    a0 = ssd_proj

    for i in range(depth):
        mod = mod_all[i]
        wi = w_in[i]
        a1 = a0 + pool_width
        qk_cols = [wi[:, a1 + t * att_w + p * LANES:a1 + t * att_w + (p + 1) * LANES]
                   for p in range(att_w // LANES) for t in range(2)]
        w_main = jnp.concatenate([wi[:, :inner + conv_ch], wi[:, a0:a1], *qk_cols, wi[:, a1 + 2 * att_w:]],
                                 axis=1).astype(BF16)
        w_dt = _pad_lanes(wi[:, inner + conv_ch:a0]).astype(BF16)
        z, xbc, dt, u, *views = _in_projection(x, mod, norm1_g[i][None], w_main, w_dt, cos, sin, widths, tm_in)
        pw_bd = jax.scipy.linalg.block_diag(*[pool_w[i, g] for g in range(pool_w.shape[1])]).astype(BF16)
        mix = _ssd_pool(
            z, xbc, dt, u, ssd_conv_w[i], ssd_conv_b[i][None], _pad_lanes(ssd_dt_bias[i][None]),
            _pad_lanes(ssd_a_log[i][None]), jnp.repeat(ssd_d[i], SSD_HEAD_DIM)[None], ssd_norm_g[i][None],
            pw_bd, pool_scale[i][None], heads, ssd_rows)
        att = _dilated_attention(views)
        x = _mix_ffn(x, mod, mix, att, w_out[i].astype(BF16), norm2_g[i][None], ffn_up[i].astype(BF16),
                     ffn_conv_w[i], ffn_conv_b[i][None], ffn_down[i].astype(BF16), final_g[None],
                     i == depth - 1, tm_ffn)
    return x
```

```python
import functools

import jax
import jax.numpy as jnp
from jax import lax
from jax.experimental import pallas as pl
from jax.experimental.pallas import tpu as pltpu

F32 = jnp.float32
BF16 = jnp.bfloat16

LANES = 128
SUBLANES = 8
VMEM_LIMIT = 56 * 1024 * 1024

NORM_EPS = 1e-6
ROPE_THETA = 500000.0
LOG2_E = 1.4426950408889634

SSD_HEAD_DIM = 64
SSD_GROUPS = 2
SSD_STATE = 128
SSD_CONV_K = 4
SSD_CHUNK = 256
POOL_WINDOWS = (2, 4, 8, 16)
ATT_HEAD_DIM = 64
ATT_PATTERNS = ((128, 1), (512, 4), (2048, 16))
ATT_BLOCK = 128
ATT_ORDER = tuple(sorted(ATT_PATTERNS, key=lambda wd: -wd[1]))
ATT_UNROLL_MAX = 4
ROT_DIM = ATT_HEAD_DIM // 4
FFN_CONV_K = 3

CONV_HALO = SUBLANES
POOL_HALO = 2 * SUBLANES


def _params(*semantics):
    return pltpu.CompilerParams(dimension_semantics=semantics, vmem_limit_bytes=VMEM_LIMIT)


def _resident(a):
    return pl.BlockSpec(a.shape, lambda *_: (0,) * a.ndim, pipeline_mode=pl.Buffered(1))


def _silu(v):
    return v * jax.nn.sigmoid(v)


def _dot(a, b):
    return jnp.dot(a, b, preferred_element_type=F32)


def _dot_nt(a, b):
    return lax.dot_general(a, b, (((1,), (1,)), ((), ())), preferred_element_type=F32)


def _ada_kernel(c_ref, w_ref, b_ref, o_ref):
    c_act = _silu(c_ref[...])
    o_ref[0] = jnp.dot(c_act, w_ref[0], preferred_element_type=F32,
                       precision=lax.Precision.HIGHEST) + b_ref[0]


def _ada_mod(c, ada_w, ada_b):
    depth, d, n = ada_w.shape
    b = c.shape[0]
    tn = d
    return pl.pallas_call(
        _ada_kernel,
        out_shape=jax.ShapeDtypeStruct((depth, b, n), F32),
        grid=(depth, n // tn),
        in_specs=[
            pl.BlockSpec((b, d), lambda i, j: (0, 0)),
            pl.BlockSpec((1, d, tn), lambda i, j: (i, 0, j)),
            pl.BlockSpec((1, 1, tn), lambda i, j: (i, 0, j)),
        ],
        out_specs=pl.BlockSpec((1, b, tn), lambda i, j: (i, 0, j)),
        compiler_params=_params("arbitrary", "arbitrary"),
        name="ada_mod",
    )(c, ada_w, ada_b.reshape(depth, 1, n))


def _rope_kernel(pos_ref, freq_ref, ecos_ref, esin_ref, cos_ref, sin_ref):
    ang = freq_ref[...] * pos_ref[0]
    rows = [jnp.cos(ang), jnp.sin(ang), jnp.ones_like(ang)]
    rows.append(jnp.zeros((LANES - len(rows) * ang.shape[0], ang.shape[1]), F32))
    t = jnp.concatenate(rows, axis=0).T
    cos_ref[0] = jnp.dot(t, ecos_ref[...], preferred_element_type=F32, precision=lax.Precision.HIGHEST)
    sin_ref[0] = jnp.dot(t, esin_ref[...], preferred_element_type=F32, precision=lax.Precision.HIGHEST)


def _rope_tables(positions):
    b, s = positions.shape
    half = ROT_DIM // 2
    assert half == SUBLANES
    inv_freq = ROPE_THETA ** (-jnp.arange(0, ROT_DIM, 2, dtype=F32) / ROT_DIM)
    lane = jnp.arange(LANES)[None, :] % ATT_HEAD_DIM
    row = jnp.arange(LANES)[:, None]
    rotary = lane < ROT_DIM
    ecos = jnp.where(rotary, row == lane % half, row == 2 * half).astype(F32)
    sign = jnp.where(lane < half, -1.0, 1.0)
    esin = jnp.where(rotary & (row == half + lane % half), sign, 0.0).astype(F32)
    return pl.pallas_call(
        _rope_kernel,
        out_shape=(jax.ShapeDtypeStruct((b, s, LANES), F32),) * 2,
        grid=(b,),
        in_specs=[
            pl.BlockSpec((1, 1, s), lambda i: (i, 0, 0)),
            pl.BlockSpec((half, 1), lambda i: (0, 0)),
            pl.BlockSpec((LANES, LANES), lambda i: (0, 0)),
            pl.BlockSpec((LANES, LANES), lambda i: (0, 0)),
        ],
        out_specs=(pl.BlockSpec((1, s, LANES), lambda i: (i, 0, 0)),) * 2,
        compiler_params=_params("arbitrary"),
        name="rope_tables",
    )(positions.astype(F32)[:, None, :], inv_freq[:, None], ecos, esin)


def _modulated_norm(x, gain, shift, scale):
    y = x * lax.rsqrt(jnp.mean(x * x, axis=-1, keepdims=True) + NORM_EPS)
    return (y * gain) * (1.0 + scale) + shift


def _rotate(t, cos, sin):
    half = ROT_DIM // 2
    lane = lax.broadcasted_iota(jnp.int32, t.shape, 1) % ATT_HEAD_DIM
    partner = jnp.where(lane < half, pltpu.roll(t, LANES - half, 1), pltpu.roll(t, half, 1))
    return t * cos + partner * sin


def _inproj_kernel(widths, x_ref, mod_ref, g_ref, w_ref, wdt_ref, cos_ref, sin_ref,
                   z_ref, xbc_ref, dt_ref, u_ref, *rest):
    qkv_refs, slabs = rest[:-1], rest[-1]
    inner, conv_ch, pool_w, att_w = widths
    tm = x_ref.shape[1]
    h = _modulated_norm(x_ref[0], g_ref[...], mod_ref[0, 0:1, :], mod_ref[0, 1:2, :]).astype(BF16)
    o = 0
    z_ref[0] = _dot(h, w_ref[:, o:o + inner]).astype(z_ref.dtype)
    o += inner
    xbc_ref[0] = _dot(h, w_ref[:, o:o + conv_ch]).astype(xbc_ref.dtype)
    o += conv_ch
    u_ref[0] = _dot(h, w_ref[:, o:o + pool_w]).astype(u_ref.dtype)
    o += pool_w
    dt_ref[0] = _dot(h, wdt_ref[...])
    cos, sin = cos_ref[0], sin_ref[0]
    scale = ATT_HEAD_DIM ** -0.5 * LOG2_E
    pairs = att_w // LANES

    def emit(t, which, p):
        slab = slabs.at[which * pairs + p]
        slab[...] = t
        for bi, (_, dil) in enumerate(ATT_ORDER):
            ref = qkv_refs[3 * bi + which]
            if dil == 1:
                ref[p, 0] = t.astype(BF16)
            else:
                for r in range(dil):
                    ref[p, 0, :, r * LANES:(r + 1) * LANES] = (
                        slab[pl.ds(r, tm // dil, stride=dil), :].astype(BF16))

    for p in range(pairs):
        qk = _dot(h, w_ref[:, o + 2 * p * LANES:o + 2 * (p + 1) * LANES])
        emit(_rotate(qk[:, :LANES], cos, sin) * scale, 0, p)
        emit(_rotate(qk[:, LANES:], cos, sin), 1, p)
    v = _dot(h, w_ref[:, o + 2 * att_w:o + 3 * att_w])
    for p in range(pairs):
        emit(v[:, p * LANES:(p + 1) * LANES], 2, p)


def _in_projection(x, mod, gain, w_main, w_dt, cos, sin, widths, tm):
    b, s, d = x.shape
    inner, conv_ch, pool_w, att_w = widths
    pairs = att_w // LANES
    row = lambda i, j: (i, j, 0)
    view_shapes, view_specs = [], []
    for _, dil in ATT_ORDER:
        assert tm % (dil * 2 * SUBLANES) == 0
        view_shapes += [jax.ShapeDtypeStruct((pairs, b, s // dil, dil * LANES), BF16)] * 3
        view_specs += [pl.BlockSpec((pairs, 1, tm // dil, dil * LANES), lambda i, j: (0, i, j, 0))] * 3
    return pl.pallas_call(
        functools.partial(_inproj_kernel, widths),
        out_shape=(
            jax.ShapeDtypeStruct((b, s, inner), BF16),
            jax.ShapeDtypeStruct((b, s, conv_ch), BF16),
            jax.ShapeDtypeStruct((b, s, LANES), F32),
            jax.ShapeDtypeStruct((b, s, pool_w), BF16),
            *view_shapes,
        ),
        grid=(b, s // tm),
        in_specs=[
            pl.BlockSpec((1, tm, d), row),
            pl.BlockSpec((1, SUBLANES, d), lambda i, j: (i, 0, 0)),
            _resident(gain), _resident(w_main), _resident(w_dt),
            pl.BlockSpec((1, tm, LANES), row),
            pl.BlockSpec((1, tm, LANES), row),
        ],
        out_specs=(
            pl.BlockSpec((1, tm, inner), row),
            pl.BlockSpec((1, tm, conv_ch), row),
            pl.BlockSpec((1, tm, LANES), row),
            pl.BlockSpec((1, tm, pool_w), row),
            *view_specs,
        ),
        scratch_shapes=[pltpu.VMEM((3 * pairs, tm, LANES), F32)],
        compiler_params=_params("arbitrary", "arbitrary"),
        name="in_projection",
    )(x, mod, gain, w_main, w_dt, cos, sin)


def _pair_rep(col, h0):
    lane = lax.broadcasted_iota(jnp.int32, (col.shape[0], LANES), 1)
    return jnp.where(lane < SSD_HEAD_DIM, col[:, h0:h0 + 1], col[:, h0 + 1:h0 + 2])


def _ssd_pool_kernel(dims, z_ref, xbc_ref, dt_ref, u_ref, cw_ref, cb_ref, dtb_ref, alog_ref, dskip_ref,
                     ng_ref, pw_ref, ps_ref, o_ref, xbuf, ubuf, hstate):
    inner, heads = dims
    q = SSD_CHUNK
    blk_rows = z_ref.shape[1]
    gw = inner // SSD_GROUPS
    n = SSD_STATE
    step = pl.program_id(1)

    @pl.when(step == 0)
    def _():
        xbuf[0:CONV_HALO, :] = jnp.zeros((CONV_HALO, xbuf.shape[1]), F32)
        ubuf[0:POOL_HALO, :] = jnp.zeros((POOL_HALO, ubuf.shape[1]), F32)
        hstate[...] = jnp.zeros(hstate.shape, F32)

    xbuf[CONV_HALO:CONV_HALO + blk_rows, :] = xbc_ref[0].astype(F32)
    ext = xbuf[...]
    conv = cb_ref[...] + cw_ref[SSD_CONV_K - 1:SSD_CONV_K, :] * ext
    for k in range(1, SSD_CONV_K):
        conv = conv + cw_ref[SSD_CONV_K - 1 - k:SSD_CONV_K - k, :] * pltpu.roll(ext, k, 0)
    xbc_all = _silu(conv[CONV_HALO:, :])
    xbuf[0:CONV_HALO, :] = ext[blk_rows:blk_rows + CONV_HALO, :]

    lane = lax.broadcasted_iota(jnp.int32, (blk_rows, LANES), 1)
    dt_all = jax.nn.softplus(dt_ref[0] + dtb_ref[...])
    a = -jnp.exp(alog_ref[...])
    da_all = jnp.where(lane < heads, dt_all * a, 0.0)
    rows = lax.broadcasted_iota(jnp.int32, (q, q), 0)
    cols = lax.broadcasted_iota(jnp.int32, (q, q), 1)
    causal = rows >= cols
    tri = jnp.where(causal, 1.0, 0.0).astype(F32)
    left = lax.broadcasted_iota(jnp.int32, (q, LANES), 1) < SSD_HEAD_DIM
    hpg = heads // SSD_GROUPS

    for ci in range(blk_rows // q):
        r0 = ci * q
        xbc = xbc_all[r0:r0 + q, :]
        dt = dt_all[r0:r0 + q, :]
        a_col = jnp.dot(tri, da_all[r0:r0 + q, :], preferred_element_type=F32,
                        precision=lax.Precision.HIGHEST)
        a_row = a_col.T
        y_groups = []
        for g in range(SSD_GROUPS):
            xs = xbc[:, g * gw:(g + 1) * gw]
            bm_f = xbc[:, inner + g * n:inner + (g + 1) * n]
            bm = bm_f.astype(BF16)
            bm_t = bm_f.T.astype(BF16)
            cm = xbc[:, inner + SSD_GROUPS * n + g * n:inner + SSD_GROUPS * n + (g + 1) * n].astype(BF16)
            cb = _dot_nt(cm, bm)
            y_off = _dot(cm, hstate[g].astype(BF16))
            y_pairs, xd_pairs, cd_pairs = [], [], []
            for p in range(hpg // 2):
                h0 = g * hpg + 2 * p
                xs_p = xs[:, p * LANES:(p + 1) * LANES]
                acum = _pair_rep(a_col, h0)
                xdt = xs_p * _pair_rep(dt, h0)
                y_p = y_off[:, p * LANES:(p + 1) * LANES] * jnp.exp(acum)
                for side in range(2):
                    hh = h0 + side
                    seg = a_col[:, hh:hh + 1] - a_row[hh:hh + 1, :]
                    lmat = jnp.exp(jnp.where(causal, seg, -jnp.inf))
                    keep = left if side == 0 else jnp.logical_not(left)
                    y_p = y_p + _dot((cb * lmat).astype(BF16), jnp.where(keep, xdt, 0.0).astype(BF16))
                a_last = acum[q - 1:q, :]
                xd_pairs.append((xdt * jnp.exp(a_last - acum)).astype(BF16))
                cd_pairs.append(jnp.exp(a_last))
                y_pairs.append(y_p + dskip_ref[:, h0 * SSD_HEAD_DIM:(h0 + 2) * SSD_HEAD_DIM] * xs_p)
            xd = jnp.concatenate(xd_pairs, axis=1)
            hstate[g] = hstate[g] * jnp.concatenate(cd_pairs, axis=1) + _dot(bm_t, xd)
            y = jnp.concatenate(y_pairs, axis=1) * _silu(z_ref[0, r0:r0 + q, g * gw:(g + 1) * gw].astype(F32))
            y = y * lax.rsqrt(jnp.mean(y * y, axis=-1, keepdims=True) + NORM_EPS)
            y_groups.append(y * ng_ref[:, g * gw:(g + 1) * gw])
        o_ref[0, r0:r0 + q, 0:inner] = jnp.concatenate(y_groups, axis=1).astype(o_ref.dtype)

    pool_w = ubuf.shape[1]
    pool_ch = pool_w // len(POOL_WINDOWS)
    u = u_ref[0].astype(F32)
    ubuf[POOL_HALO:POOL_HALO + blk_rows, :] = u
    ext = ubuf[...]
    plane = lax.broadcasted_iota(jnp.int32, (blk_rows, pool_w), 1)
    pos = step * blk_rows + lax.broadcasted_iota(jnp.int32, (blk_rows, pool_w), 0)
    pooled = jnp.zeros((blk_rows, pool_w), F32)
    win = jnp.zeros((blk_rows, pool_w), jnp.int32)
    acc, width = ext, 1
    for gi, w in enumerate(POOL_WINDOWS):
        while width < w:
            acc = acc + pltpu.roll(acc, width, 0)
            width *= 2
        assert width == w
        in_group = (plane >= gi * pool_ch) & (plane < (gi + 1) * pool_ch)
        pooled = jnp.where(in_group, acc[POOL_HALO:, :], pooled)
        win = jnp.where(in_group, w, win)
    cnt = jnp.minimum(pos + 1, win).astype(F32)
    diff = (pooled / cnt - u).astype(BF16)
    ubuf[0:POOL_HALO, :] = ext[blk_rows:blk_rows + POOL_HALO, :]
    o_ref[0, :, inner:inner + pool_w] = (_dot(diff, pw_ref[...]) * ps_ref[...]).astype(o_ref.dtype)


def _ssd_pool(z, xbc, dt, u, cw, cb, dtb, alog, dskip, ng, pw, ps, heads, blk_rows):
    b, s, inner = z.shape
    conv_ch = xbc.shape[-1]
    pool_w = u.shape[-1]
    assert blk_rows % SSD_CHUNK == 0 and s % blk_rows == 0
    row = lambda i, j: (i, j, 0)
    full = _resident
    out_w = inner + pool_w
    return pl.pallas_call(
        functools.partial(_ssd_pool_kernel, (inner, heads)),
        out_shape=jax.ShapeDtypeStruct((b, s, out_w), BF16),
        grid=(b, s // blk_rows),
        in_specs=[
            pl.BlockSpec((1, blk_rows, inner), row),
            pl.BlockSpec((1, blk_rows, conv_ch), row),
            pl.BlockSpec((1, blk_rows, LANES), row),
            pl.BlockSpec((1, blk_rows, pool_w), row),
            full(cw), full(cb), full(dtb), full(alog), full(dskip), full(ng), full(pw), full(ps),
        ],
        out_specs=pl.BlockSpec((1, blk_rows, out_w), row),
        scratch_shapes=[
            pltpu.VMEM((CONV_HALO + blk_rows, conv_ch), F32),
            pltpu.VMEM((POOL_HALO + blk_rows, pool_w), F32),
            pltpu.VMEM((SSD_GROUPS, SSD_STATE, inner // SSD_GROUPS), F32),
        ],
        compiler_params=_params("arbitrary", "arbitrary"),
        name="ssd_pool",
    )(z, xbc, dt, u, cw, cb, dtb, alog, dskip, ng, pw, ps)


def _attn_kernel(seq, *refs):
    nbr = len(ATT_ORDER)
    qkv_refs = refs[:3 * nbr]
    o_ref = refs[3 * nbr]
    acc_ref, l_ref, m0_ref, m1_ref = refs[3 * nbr + 1:]
    blk = ATT_BLOCK
    hd = ATT_HEAD_DIM
    lane = lax.broadcasted_iota(jnp.int32, (blk, LANES), 1)
    left = lane < hd
    qi = lax.broadcasted_iota(jnp.int32, (blk, blk), 0)
    kj = lax.broadcasted_iota(jnp.int32, (blk, blk), 1)
    mask_cur = kj <= qi
    mask_prev = kj >= qi
    neg = -jnp.inf

    mask_prev2 = jnp.concatenate([mask_prev, mask_prev], axis=0)
    mask_cur2 = jnp.concatenate([mask_cur, mask_cur], axis=0)

    def softmax_blocks(qs, kbands, vbands, pens, states):
        nb = len(qs)
        zero = jnp.zeros_like(qs[0])
        scores = []
        for u in range(nb):
            q2 = jnp.concatenate([jnp.where(left, qs[u], zero), jnp.where(left, zero, qs[u])], axis=0)
            s = _dot_nt(q2, kbands[u])
            s_p = s[:, :blk] if pens[u] is None else s[:, :blk] + pens[u]
            scores.append((jnp.where(mask_prev2, s_p, neg), jnp.where(mask_cur2, s[:, blk:], neg)))
        m_new = []
        for u in range(nb):
            s_p, s_c = scores[u]
            m_blk = jnp.max(jnp.maximum(s_p, s_c), axis=-1, keepdims=True)
            if states[u] is None:
                m_new.append(jnp.broadcast_to(m_blk, (2 * blk, LANES)))
            else:
                m_new.append(jnp.maximum(jnp.concatenate(states[u][:2], axis=0), m_blk))
        res, sums = [], []
        for u in range(nb):
            s_p, s_c = scores[u]
            p_p, p_c = jnp.exp2(s_p - m_new[u]), jnp.exp2(s_c - m_new[u])
            sums.append(jnp.sum(p_p + p_c, axis=-1, keepdims=True))
            res.append(_dot(jnp.concatenate([p_p, p_c], axis=1).astype(BF16), vbands[u]))
        outs = []
        for u in range(nb):
            acc_new = jnp.where(left, res[u][:blk], res[u][blk:])
            l_new = jnp.where(left, sums[u][:blk], sums[u][blk:])
            m0_new, m1_new = m_new[u][:blk], m_new[u][blk:]
            if states[u] is not None:
                alpha = jnp.where(left, jnp.exp2(states[u][0] - m0_new), jnp.exp2(states[u][1] - m1_new))
                acc_new = alpha * states[u][2] + acc_new
                l_new = alpha * states[u][3] + l_new
            outs.append((m0_new, m1_new, acc_new, l_new))
        return outs

    for bi, (window, dil) in enumerate(ATT_ORDER):
        assert window // dil == blk
        q_ref, k_ref, v_ref = qkv_refs[3 * bi:3 * bi + 3]
        nblk = seq // dil // blk
        first = bi == 0
        last = bi == nbr - 1
        unroll = min(ATT_UNROLL_MAX, nblk)
        assert nblk % unroll == 0 and (not last or dil == 1)
        for r in range(dil):
            col = r * LANES

            def body(it, carry, first=first, last=last, dil=dil, r=r, col=col, unroll=unroll,
                     q_ref=q_ref, k_ref=k_ref, v_ref=v_ref):
                n0 = it * unroll

                def tile(ref, n):
                    if isinstance(n, int):
                        return ref[0, 0, n * blk:(n + 1) * blk, col:col + LANES]
                    return ref[0, 0, pl.ds(pl.multiple_of(n * blk, blk), blk), col:col + LANES]

                def state_rows(n):
                    if dil == 1:
                        return pl.ds(n * blk if isinstance(n, int) else pl.multiple_of(n * blk, blk), blk)
                    return pl.ds(n * (blk * dil) + r, blk, stride=dil)

                n_prev = max(n0 - 1, 0) if isinstance(n0, int) else jnp.maximum(n0 - 1, 0)
                ks = [tile(k_ref, n_prev)] + [tile(k_ref, n0 + u) for u in range(unroll)]
                vs = [tile(v_ref, n_prev)] + [tile(v_ref, n0 + u) for u in range(unroll)]
                qs = [tile(q_ref, n0 + u) for u in range(unroll)]
                rows = [state_rows(n0 + u) for u in range(unroll)]
                states = [None if first else (m0_ref[rw, :], m1_ref[rw, :], acc_ref[rw, :], l_ref[rw, :])
                          for rw in rows]
                if isinstance(n0, int):
                    pen0 = None if n0 > 0 else neg
                else:
                    pen0 = jnp.where(n0 > 0, 0.0, neg).astype(F32)
                kbands = [jnp.concatenate([ks[u], ks[u + 1]], axis=0) for u in range(unroll)]
                vbands = [jnp.concatenate([vs[u], vs[u + 1]], axis=0) for u in range(unroll)]
                outs = softmax_blocks(qs, kbands, vbands, [pen0] + [None] * (unroll - 1), states)
                for rw, (m0_new, m1_new, acc_new, l_new) in zip(rows, outs):
                    if last:
                        o_ref[0, 0, rw, :] = (acc_new / l_new).astype(o_ref.dtype)
                    else:
                        m0_ref[rw, :] = m0_new
                        m1_ref[rw, :] = m1_new
                        acc_ref[rw, :] = acc_new
                        l_ref[rw, :] = l_new
                return carry

            if nblk == unroll:
                body(0, 0)
            else:
                lax.fori_loop(0, nblk // unroll, body, 0)


def _dilated_attention(views):
    pairs, b = views[-1].shape[:2]
    s = views[-1].shape[2] * ATT_ORDER[-1][1]
    specs = [pl.BlockSpec((1, 1) + t.shape[2:], lambda i, j: (j, i, 0, 0)) for t in views]
    return pl.pallas_call(
        functools.partial(_attn_kernel, s),
        out_shape=jax.ShapeDtypeStruct((pairs, b, s, LANES), BF16),
        grid=(b, pairs),
        in_specs=specs,
        out_specs=pl.BlockSpec((1, 1, s, LANES), lambda i, j: (j, i, 0, 0)),
        scratch_shapes=[pltpu.VMEM((s, LANES), F32)] * 4,
        compiler_params=_params("arbitrary", "arbitrary"),
        name="dilated_attention",
    )(*views)


FFN_COLS = 256
OUT_COLS = 256


def _mix_ffn_kernel(cfg, x_ref, mod_ref, mix_ref, att_ref, wout_ref, g_ref, up_ref, cw_ref, cb_ref, down_ref,
                    fg_ref, o_ref, x1, carry, act):
    ffn, final_norm = cfg
    tm, d = x_ref.shape[1:]
    fc = FFN_COLS
    halo = SUBLANES
    mw = mix_ref.shape[-1]

    for c0 in range(0, d, OUT_COLS):
        y = _dot(mix_ref[0], wout_ref[0:mw, c0:c0 + OUT_COLS])
        for p in range(att_ref.shape[0]):
            y = y + _dot(att_ref[p, 0], wout_ref[mw + p * LANES:mw + (p + 1) * LANES, c0:c0 + OUT_COLS])
        x1[:, c0:c0 + OUT_COLS] = x_ref[0, :, c0:c0 + OUT_COLS] + mod_ref[0, 2:3, c0:c0 + OUT_COLS] * y

    h = _modulated_norm(x1[...], g_ref[...], mod_ref[0, 3:4, :], mod_ref[0, 4:5, :]).astype(BF16)

    @pl.when(pl.program_id(1) == 0)
    def _():
        carry[...] = jnp.zeros(carry.shape, F32)

    top_row = lax.broadcasted_iota(jnp.int32, (halo, fc), 0)

    def conv_act(c0):
        hid = _dot(h, up_ref[:, c0:c0 + fc])
        prev = carry[:, c0:c0 + fc]
        carry[:, c0:c0 + fc] = hid[tm - halo:tm, :]
        out = cb_ref[:, c0:c0 + fc] + cw_ref[FFN_CONV_K - 1:FFN_CONV_K, c0:c0 + fc] * hid
        for k in range(1, FFN_CONV_K):
            sh = pltpu.roll(hid, k, 0)
            top = jnp.where(top_row < k, pltpu.roll(prev, k, 0), sh[0:halo, :])
            sh = jnp.concatenate([top, sh[halo:, :]], axis=0)
            out = out + cw_ref[FFN_CONV_K - 1 - k:FFN_CONV_K - k, c0:c0 + fc] * sh
        return out

    for j in range(ffn // fc):
        gate = conv_act(j * fc)
        val = conv_act(ffn + j * fc)
        act[:, j * fc:(j + 1) * fc] = (_silu(gate) * val).astype(BF16)

    for c0 in range(0, d, OUT_COLS):
        y = _dot(act[...], down_ref[:, c0:c0 + OUT_COLS])
        o_ref[0, :, c0:c0 + OUT_COLS] = x1[:, c0:c0 + OUT_COLS] + mod_ref[0, 5:6, c0:c0 + OUT_COLS] * y
    if final_norm:
        y = o_ref[0]
        o_ref[0] = y * lax.rsqrt(jnp.mean(y * y, axis=-1, keepdims=True) + NORM_EPS) * fg_ref[...]


def _mix_ffn(x, mod, mix, att, w_out, gain, up, cw, cb, down, final_g, final_norm, tm):
    b, s, d = x.shape
    ffn = down.shape[0]
    pairs = att.shape[0]
    assert ffn % FFN_COLS == 0 and d % OUT_COLS == 0
    row = lambda i, j: (i, j, 0)
    return pl.pallas_call(
        functools.partial(_mix_ffn_kernel, (ffn, final_norm)),
        out_shape=jax.ShapeDtypeStruct((b, s, d), F32),
        grid=(b, s // tm),
        in_specs=[
            pl.BlockSpec((1, tm, d), row),
            pl.BlockSpec((1, SUBLANES, d), lambda i, j: (i, 0, 0)),
            pl.BlockSpec((1, tm, mix.shape[-1]), row),
            pl.BlockSpec((pairs, 1, tm, LANES), lambda i, j: (0, i, j, 0)),
            _resident(w_out), _resident(gain), _resident(up), _resident(cw), _resident(cb), _resident(down),
            _resident(final_g),
        ],
        out_specs=pl.BlockSpec((1, tm, d), row),
        scratch_shapes=[
            pltpu.VMEM((tm, d), F32),
            pltpu.VMEM((SUBLANES, 2 * ffn), F32),
            pltpu.VMEM((tm, ffn), BF16),
        ],
        compiler_params=_params("arbitrary", "arbitrary"),
        name="mix_ffn",
    )(x, mod, mix, att, w_out, gain, up, cw, cb, down, final_g)


def _tile_rows(s):
    pick = lambda top: max(t for t in (256, 512, 1024) if t <= top and s % t == 0)
    return pick(1024), pick(512), pick(1024)


def _pad_lanes(v, width=LANES):
    return jnp.pad(v, ((0, 0), (0, width - v.shape[-1])))


def kernel(x, c, positions, ada_w, ada_b, norm1_g, w_in, ssd_conv_w, ssd_conv_b, ssd_dt_bias, ssd_a_log,
           ssd_d, ssd_norm_g, pool_w, pool_scale, w_out, norm2_g, ffn_up, ffn_conv_w, ffn_conv_b, ffn_down,
           final_g):
    depth = w_in.shape[0]
    b, s, d = x.shape
    heads = ssd_dt_bias.shape[-1]
    inner = heads * SSD_HEAD_DIM
    conv_ch = ssd_conv_w.shape[-1]
    pool_width = pool_scale.shape[-1]
    ssd_proj = inner + conv_ch + heads
    att_w = (w_in.shape[-1] - ssd_proj - pool_width) // 3
    widths = (inner, conv_ch, pool_width, att_w)
    assert conv_ch == inner + 2 * SSD_GROUPS * SSD_STATE and att_w % LANES == 0
    assert s % (ATT_BLOCK * ATT_PATTERNS[-1][1]) == 0 and s % SSD_CHUNK == 0
    tm_in, tm_ffn, ssd_rows = _tile_rows(s)

    mod_all = _ada_mod(c, ada_w, ada_b).reshape(depth, b, 6, d)
    mod_all = jnp.pad(mod_all, ((0, 0), (0, 0), (0, SUBLANES - 6), (0, 0)))
    cos, sin = _rope_tables(positions)
    a0 = ssd_proj

    for i in range(depth):
        mod = mod_all[i]
        wi = w_in[i]
        a1 = a0 + pool_width
        qk_cols = [wi[:, a1 + t * att_w + p * LANES:a1 + t * att_w + (p + 1) * LANES]
                   for p in range(att_w // LANES) for t in range(2)]
        w_main = jnp.concatenate([wi[:, :inner + conv_ch], wi[:, a0:a1], *qk_cols, wi[:, a1 + 2 * att_w:]],
                                 axis=1).astype(BF16)
        w_dt = _pad_lanes(wi[:, inner + conv_ch:a0]).astype(BF16)
        z, xbc, dt, u, *views = _in_projection(x, mod, norm1_g[i][None], w_main, w_dt, cos, sin, widths, tm_in)
        pw_bd = jax.scipy.linalg.block_diag(*[pool_w[i, g] for g in range(pool_w.shape[1])]).astype(BF16)
        mix = _ssd_pool(
            z, xbc, dt, u, ssd_conv_w[i], ssd_conv_b[i][None], _pad_lanes(ssd_dt_bias[i][None]),
            _pad_lanes(ssd_a_log[i][None]), jnp.repeat(ssd_d[i], SSD_HEAD_DIM)[None], ssd_norm_g[i][None],
            pw_bd, pool_scale[i][None], heads, ssd_rows)
        att = _dilated_attention(views)
        x = _mix_ffn(x, mod, mix, att, w_out[i].astype(BF16), norm2_g[i][None], ffn_up[i].astype(BF16),
                     ffn_conv_w[i], ffn_conv_b[i][None], ffn_down[i].astype(BF16), final_g[None],
                     i == depth - 1, tm_ffn)
    return x
```

```python
import functools

import jax
import jax.numpy as jnp
from jax import lax
from jax.experimental import pallas as pl
from jax.experimental.pallas import tpu as pltpu

F32 = jnp.float32
BF16 = jnp.bfloat16

LANES = 128
SUBLANES = 8
VMEM_LIMIT = 56 * 1024 * 1024

NORM_EPS = 1e-6
ROPE_THETA = 500000.0
LOG2_E = 1.4426950408889634

SSD_HEAD_DIM = 64
SSD_GROUPS = 2
SSD_STATE = 128
SSD_CONV_K = 4
SSD_CHUNK = 256
POOL_WINDOWS = (2, 4, 8, 16)
ATT_HEAD_DIM = 64
ATT_PATTERNS = ((128, 1), (512, 4), (2048, 16))
ATT_BLOCK = 128
ATT_ORDER = tuple(sorted(ATT_PATTERNS, key=lambda wd: -wd[1]))
ATT_UNROLL_MAX = 4
ROT_DIM = ATT_HEAD_DIM // 4
FFN_CONV_K = 3

POOL_HALO = 2 * SUBLANES
INPROJ_CONV_COLS = 256


def _params(*semantics):
    return pltpu.CompilerParams(dimension_semantics=semantics, vmem_limit_bytes=VMEM_LIMIT)


def _resident(a):
    return pl.BlockSpec(a.shape, lambda *_: (0,) * a.ndim, pipeline_mode=pl.Buffered(1))


def _silu(v):
    return v * jax.nn.sigmoid(v)


def _dot(a, b):
    return jnp.dot(a, b, preferred_element_type=F32)


def _dot_nt(a, b):
    return lax.dot_general(a, b, (((1,), (1,)), ((), ())), preferred_element_type=F32)


def _ada_kernel(c_ref, w_ref, b_ref, o_ref):
    c_act = _silu(c_ref[...])
    o_ref[0] = jnp.dot(c_act, w_ref[0], preferred_element_type=F32,
                       precision=lax.Precision.HIGHEST) + b_ref[0]


def _ada_mod(c, ada_w, ada_b):
    depth, d, n = ada_w.shape
    b = c.shape[0]
    tn = d
    return pl.pallas_call(
        _ada_kernel,
        out_shape=jax.ShapeDtypeStruct((depth, b, n), F32),
        grid=(depth, n // tn),
        in_specs=[
            pl.BlockSpec((b, d), lambda i, j: (0, 0)),
            pl.BlockSpec((1, d, tn), lambda i, j: (i, 0, j)),
            pl.BlockSpec((1, 1, tn), lambda i, j: (i, 0, j)),
        ],
        out_specs=pl.BlockSpec((1, b, tn), lambda i, j: (i, 0, j)),
        compiler_params=_params("arbitrary", "arbitrary"),
        name="ada_mod",
    )(c, ada_w, ada_b.reshape(depth, 1, n))


def _rope_kernel(pos_ref, freq_ref, ecos_ref, esin_ref, cos_ref, sin_ref):
    ang = freq_ref[...] * pos_ref[0]
    rows = [jnp.cos(ang), jnp.sin(ang), jnp.ones_like(ang)]
    rows.append(jnp.zeros((LANES - len(rows) * ang.shape[0], ang.shape[1]), F32))
    t = jnp.concatenate(rows, axis=0).T
    cos_ref[0] = jnp.dot(t, ecos_ref[...], preferred_element_type=F32, precision=lax.Precision.HIGHEST)
    sin_ref[0] = jnp.dot(t, esin_ref[...], preferred_element_type=F32, precision=lax.Precision.HIGHEST)


def _rope_tables(positions):
    b, s = positions.shape
    half = ROT_DIM // 2
    assert half == SUBLANES
    inv_freq = ROPE_THETA ** (-jnp.arange(0, ROT_DIM, 2, dtype=F32) / ROT_DIM)
    lane = jnp.arange(LANES)[None, :] % ATT_HEAD_DIM
    row = jnp.arange(LANES)[:, None]
    rotary = lane < ROT_DIM
    ecos = jnp.where(rotary, row == lane % half, row == 2 * half).astype(F32)
    sign = jnp.where(lane < half, -1.0, 1.0)
    esin = jnp.where(rotary & (row == half + lane % half), sign, 0.0).astype(F32)
    return pl.pallas_call(
        _rope_kernel,
        out_shape=(jax.ShapeDtypeStruct((b, s, LANES), F32),) * 2,
        grid=(b,),
        in_specs=[
            pl.BlockSpec((1, 1, s), lambda i: (i, 0, 0)),
            pl.BlockSpec((half, 1), lambda i: (0, 0)),
            pl.BlockSpec((LANES, LANES), lambda i: (0, 0)),
            pl.BlockSpec((LANES, LANES), lambda i: (0, 0)),
        ],
        out_specs=(pl.BlockSpec((1, s, LANES), lambda i: (i, 0, 0)),) * 2,
        compiler_params=_params("arbitrary"),
        name="rope_tables",
    )(positions.astype(F32)[:, None, :], inv_freq[:, None], ecos, esin)


def _modulated_norm(x, gain, shift, scale):
    y = x * lax.rsqrt(jnp.mean(x * x, axis=-1, keepdims=True) + NORM_EPS)
    return (y * gain) * (1.0 + scale) + shift


def _rotate(t, cos, sin):
    half = ROT_DIM // 2
    lane = lax.broadcasted_iota(jnp.int32, t.shape, 1) % ATT_HEAD_DIM
    partner = jnp.where(lane < half, pltpu.roll(t, LANES - half, 1), pltpu.roll(t, half, 1))
    return t * cos + partner * sin


def _causal_dwconv(hid, prev, w_ref, b_ref, c0, taps):
    cols = hid.shape[1]
    top_row = lax.broadcasted_iota(jnp.int32, (SUBLANES, cols), 0)
    out = b_ref[:, c0:c0 + cols] + w_ref[taps - 1:taps, c0:c0 + cols] * hid
    for k in range(1, taps):
        sh = pltpu.roll(hid, k, 0)
        top = jnp.where(top_row < k, pltpu.roll(prev, k, 0), sh[0:SUBLANES, :])
        sh = jnp.concatenate([top, sh[SUBLANES:, :]], axis=0)
        out = out + w_ref[taps - 1 - k:taps - k, c0:c0 + cols] * sh
    return out


def _inproj_kernel(widths, x_ref, mod_ref, g_ref, w_ref, wdt_ref, cos_ref, sin_ref, cw_ref, cb_ref,
                   z_ref, xbc_ref, dt_ref, u_ref, *rest):
    qkv_refs, slabs, carry = rest[:-2], rest[-2], rest[-1]
    inner, conv_ch, pool_w, att_w = widths
    tm = x_ref.shape[1]
    h = _modulated_norm(x_ref[0], g_ref[...], mod_ref[0, 0:1, :], mod_ref[0, 1:2, :]).astype(BF16)
    o = 0
    z_ref[0] = _silu(_dot(h, w_ref[:, o:o + inner])).astype(z_ref.dtype)
    o += inner

    @pl.when(pl.program_id(1) == 0)
    def _():
        carry[...] = jnp.zeros(carry.shape, F32)

    for c0 in range(0, conv_ch, INPROJ_CONV_COLS):
        hid = _dot(h, w_ref[:, o + c0:o + c0 + INPROJ_CONV_COLS])
        prev = carry[:, c0:c0 + INPROJ_CONV_COLS]
        carry[:, c0:c0 + INPROJ_CONV_COLS] = hid[tm - SUBLANES:tm, :]
        conv = _causal_dwconv(hid, prev, cw_ref, cb_ref, c0, SSD_CONV_K)
        xbc_ref[0, :, c0:c0 + INPROJ_CONV_COLS] = _silu(conv).astype(xbc_ref.dtype)
    o += conv_ch
    u_ref[0] = _dot(h, w_ref[:, o:o + pool_w]).astype(u_ref.dtype)
    o += pool_w
    dt_ref[0] = _dot(h, wdt_ref[...])
    cos, sin = cos_ref[0], sin_ref[0]
    scale = ATT_HEAD_DIM ** -0.5 * LOG2_E
    pairs = att_w // LANES

    def emit(t, which, p):
        slab = slabs.at[which * pairs + p]
        slab[...] = t
        for bi, (_, dil) in enumerate(ATT_ORDER):
            ref = qkv_refs[3 * bi + which]
            if dil == 1:
                ref[p, 0] = t.astype(BF16)
            else:
                for r in range(dil):
                    ref[p, 0, :, r * LANES:(r + 1) * LANES] = (
                        slab[pl.ds(r, tm // dil, stride=dil), :].astype(BF16))

    for p in range(pairs):
        qk = _dot(h, w_ref[:, o + 2 * p * LANES:o + 2 * (p + 1) * LANES])
        emit(_rotate(qk[:, :LANES], cos, sin) * scale, 0, p)
        emit(_rotate(qk[:, LANES:], cos, sin), 1, p)
    v = _dot(h, w_ref[:, o + 2 * att_w:o + 3 * att_w])
    for p in range(pairs):
        emit(v[:, p * LANES:(p + 1) * LANES], 2, p)


def _in_projection(x, mod, gain, w_main, w_dt, cos, sin, conv_w, conv_b, widths, tm):
    b, s, d = x.shape
    inner, conv_ch, pool_w, att_w = widths
    pairs = att_w // LANES
    assert conv_ch % INPROJ_CONV_COLS == 0
    row = lambda i, j: (i, j, 0)
    view_shapes, view_specs = [], []
    for _, dil in ATT_ORDER:
        assert tm % (dil * 2 * SUBLANES) == 0
        view_shapes += [jax.ShapeDtypeStruct((pairs, b, s // dil, dil * LANES), BF16)] * 3
        view_specs += [pl.BlockSpec((pairs, 1, tm // dil, dil * LANES), lambda i, j: (0, i, j, 0))] * 3
    return pl.pallas_call(
        functools.partial(_inproj_kernel, widths),
        out_shape=(
            jax.ShapeDtypeStruct((b, s, inner), BF16),
            jax.ShapeDtypeStruct((b, s, conv_ch), BF16),
            jax.ShapeDtypeStruct((b, s, LANES), F32),
            jax.ShapeDtypeStruct((b, s, pool_w), BF16),
            *view_shapes,
        ),
        grid=(b, s // tm),
        in_specs=[
            pl.BlockSpec((1, tm, d), row),
            pl.BlockSpec((1, SUBLANES, d), lambda i, j: (i, 0, 0)),
            _resident(gain), _resident(w_main), _resident(w_dt),
            pl.BlockSpec((1, tm, LANES), row),
            pl.BlockSpec((1, tm, LANES), row),
            _resident(conv_w), _resident(conv_b),
        ],
        out_specs=(
            pl.BlockSpec((1, tm, inner), row),
            pl.BlockSpec((1, tm, conv_ch), row),
            pl.BlockSpec((1, tm, LANES), row),
            pl.BlockSpec((1, tm, pool_w), row),
            *view_specs,
        ),
        scratch_shapes=[pltpu.VMEM((3 * pairs, tm, LANES), F32), pltpu.VMEM((SUBLANES, conv_ch), F32)],
        compiler_params=_params("arbitrary", "arbitrary"),
        name="in_projection",
    )(x, mod, gain, w_main, w_dt, cos, sin, conv_w, conv_b)


def _pair_rep(col, h0):
    lane = lax.broadcasted_iota(jnp.int32, (col.shape[0], LANES), 1)
    return jnp.where(lane < SSD_HEAD_DIM, col[:, h0:h0 + 1], col[:, h0 + 1:h0 + 2])


def _ssd_pool_kernel(dims, z_ref, xbc_ref, dt_ref, u_ref, dtb_ref, alog_ref, dskip_ref,
                     ng_ref, pw_ref, ps_ref, o_ref, ubuf, hstate):
    inner, heads = dims
    q = SSD_CHUNK
    blk_rows = z_ref.shape[1]
    gw = inner // SSD_GROUPS
    n = SSD_STATE
    step = pl.program_id(1)

    @pl.when(step == 0)
    def _():
        ubuf[0:POOL_HALO, :] = jnp.zeros((POOL_HALO, ubuf.shape[1]), F32)
        hstate[...] = jnp.zeros(hstate.shape, F32)

    lane = lax.broadcasted_iota(jnp.int32, (blk_rows, LANES), 1)
    dt_all = jax.nn.softplus(dt_ref[0] + dtb_ref[...])
    a = -jnp.exp(alog_ref[...])
    da_all = jnp.where(lane < heads, dt_all * a, 0.0)
    rows = lax.broadcasted_iota(jnp.int32, (q, q), 0)
    cols = lax.broadcasted_iota(jnp.int32, (q, q), 1)
    causal = rows >= cols
    tri = jnp.where(causal, 1.0, 0.0).astype(F32)
    left = lax.broadcasted_iota(jnp.int32, (q, LANES), 1) < SSD_HEAD_DIM
    hpg = heads // SSD_GROUPS

    for ci in range(blk_rows // q):
        r0 = ci * q
        xbc = xbc_ref[0, r0:r0 + q, :].astype(F32)
        dt = dt_all[r0:r0 + q, :]
        a_col = jnp.dot(tri, da_all[r0:r0 + q, :], preferred_element_type=F32,
                        precision=lax.Precision.HIGHEST)
        a_row = a_col.T
        y_groups = []
        for g in range(SSD_GROUPS):
            xs = xbc[:, g * gw:(g + 1) * gw]
            bm_f = xbc[:, inner + g * n:inner + (g + 1) * n]
            bm = bm_f.astype(BF16)
            bm_t = bm_f.T.astype(BF16)
            cm = xbc[:, inner + SSD_GROUPS * n + g * n:inner + SSD_GROUPS * n + (g + 1) * n].astype(BF16)
            cb = _dot_nt(cm, bm)
            y_off = _dot(cm, hstate[g].astype(BF16))
            y_pairs, xd_pairs, cd_pairs = [], [], []
            for p in range(hpg // 2):
                h0 = g * hpg + 2 * p
                xs_p = xs[:, p * LANES:(p + 1) * LANES]
                acum = _pair_rep(a_col, h0)
                xdt = xs_p * _pair_rep(dt, h0)
                y_p = y_off[:, p * LANES:(p + 1) * LANES] * jnp.exp(acum)
                for side in range(2):
                    hh = h0 + side
                    seg = a_col[:, hh:hh + 1] - a_row[hh:hh + 1, :]
                    lmat = jnp.exp(jnp.where(causal, seg, -jnp.inf))
                    keep = left if side == 0 else jnp.logical_not(left)
                    y_p = y_p + _dot((cb * lmat).astype(BF16), jnp.where(keep, xdt, 0.0).astype(BF16))
                a_last = acum[q - 1:q, :]
                xd_pairs.append((xdt * jnp.exp(a_last - acum)).astype(BF16))
                cd_pairs.append(jnp.exp(a_last))
                y_pairs.append(y_p + dskip_ref[:, h0 * SSD_HEAD_DIM:(h0 + 2) * SSD_HEAD_DIM] * xs_p)
            xd = jnp.concatenate(xd_pairs, axis=1)
            hstate[g] = hstate[g] * jnp.concatenate(cd_pairs, axis=1) + _dot(bm_t, xd)
            y = jnp.concatenate(y_pairs, axis=1) * z_ref[0, r0:r0 + q, g * gw:(g + 1) * gw].astype(F32)
            y = y * lax.rsqrt(jnp.mean(y * y, axis=-1, keepdims=True) + NORM_EPS)
            y_groups.append(y * ng_ref[:, g * gw:(g + 1) * gw])
        o_ref[0, r0:r0 + q, 0:inner] = jnp.concatenate(y_groups, axis=1).astype(o_ref.dtype)

    pool_w = ubuf.shape[1]
    pool_ch = pool_w // len(POOL_WINDOWS)
    u = u_ref[0].astype(F32)
    ubuf[POOL_HALO:POOL_HALO + blk_rows, :] = u
    ext = ubuf[...]
    plane = lax.broadcasted_iota(jnp.int32, (blk_rows, pool_w), 1)
    pos = step * blk_rows + lax.broadcasted_iota(jnp.int32, (blk_rows, pool_w), 0)
    pooled = jnp.zeros((blk_rows, pool_w), F32)
    win = jnp.zeros((blk_rows, pool_w), jnp.int32)
    acc, width = ext, 1
    for gi, w in enumerate(POOL_WINDOWS):
        while width < w:
            acc = acc + pltpu.roll(acc, width, 0)
            width *= 2
        assert width == w
        in_group = (plane >= gi * pool_ch) & (plane < (gi + 1) * pool_ch)
        pooled = jnp.where(in_group, acc[POOL_HALO:, :], pooled)
        win = jnp.where(in_group, w, win)
    cnt = jnp.minimum(pos + 1, win).astype(F32)
    diff = (pooled / cnt - u).astype(BF16)
    ubuf[0:POOL_HALO, :] = ext[blk_rows:blk_rows + POOL_HALO, :]
    o_ref[0, :, inner:inner + pool_w] = (_dot(diff, pw_ref[...]) * ps_ref[...]).astype(o_ref.dtype)


def _ssd_pool(z, xbc, dt, u, dtb, alog, dskip, ng, pw, ps, heads, blk_rows):
    b, s, inner = z.shape
    conv_ch = xbc.shape[-1]
    pool_w = u.shape[-1]
    assert blk_rows % SSD_CHUNK == 0 and s % blk_rows == 0
    row = lambda i, j: (i, j, 0)
    full = _resident
    out_w = inner + pool_w
    return pl.pallas_call(
        functools.partial(_ssd_pool_kernel, (inner, heads)),
        out_shape=jax.ShapeDtypeStruct((b, s, out_w), BF16),
        grid=(b, s // blk_rows),
        in_specs=[
            pl.BlockSpec((1, blk_rows, inner), row),
            pl.BlockSpec((1, blk_rows, conv_ch), row),
            pl.BlockSpec((1, blk_rows, LANES), row),
            pl.BlockSpec((1, blk_rows, pool_w), row),
            full(dtb), full(alog), full(dskip), full(ng), full(pw), full(ps),
        ],
        out_specs=pl.BlockSpec((1, blk_rows, out_w), row),
        scratch_shapes=[
            pltpu.VMEM((POOL_HALO + blk_rows, pool_w), F32),
            pltpu.VMEM((SSD_GROUPS, SSD_STATE, inner // SSD_GROUPS), F32),
        ],
        compiler_params=_params("arbitrary", "arbitrary"),
        name="ssd_pool",
    )(z, xbc, dt, u, dtb, alog, dskip, ng, pw, ps)


def _attn_kernel(seq, *refs):
    nbr = len(ATT_ORDER)
    qkv_refs = refs[:3 * nbr]
    o_ref = refs[3 * nbr]
    acc_ref, l_ref, m0_ref, m1_ref = refs[3 * nbr + 1:]
    blk = ATT_BLOCK
    hd = ATT_HEAD_DIM
    lane = lax.broadcasted_iota(jnp.int32, (blk, LANES), 1)
    left = lane < hd
    qi = lax.broadcasted_iota(jnp.int32, (blk, blk), 0)
    kj = lax.broadcasted_iota(jnp.int32, (blk, blk), 1)
    mask_cur = kj <= qi
    mask_prev = kj >= qi
    neg = -jnp.inf

    mask_prev2 = jnp.concatenate([mask_prev, mask_prev], axis=0)
    mask_cur2 = jnp.concatenate([mask_cur, mask_cur], axis=0)

    def softmax_blocks(qs, kbands, vbands, pens, states):
        nb = len(qs)
        zero = jnp.zeros_like(qs[0])
        scores = []
        for u in range(nb):
            q2 = jnp.concatenate([jnp.where(left, qs[u], zero), jnp.where(left, zero, qs[u])], axis=0)
            s = _dot_nt(q2, kbands[u])
            s_p = s[:, :blk] if pens[u] is None else s[:, :blk] + pens[u]
            scores.append((jnp.where(mask_prev2, s_p, neg), jnp.where(mask_cur2, s[:, blk:], neg)))
        m_new = []
        for u in range(nb):
            s_p, s_c = scores[u]
            m_blk = jnp.max(jnp.maximum(s_p, s_c), axis=-1, keepdims=True)
            if states[u] is None:
                m_new.append(jnp.broadcast_to(m_blk, (2 * blk, LANES)))
            else:
                m_new.append(jnp.maximum(jnp.concatenate(states[u][:2], axis=0), m_blk))
        res, sums = [], []
        for u in range(nb):
            s_p, s_c = scores[u]
            p_p, p_c = jnp.exp2(s_p - m_new[u]), jnp.exp2(s_c - m_new[u])
            sums.append(jnp.sum(p_p + p_c, axis=-1, keepdims=True))
            res.append(_dot(jnp.concatenate([p_p, p_c], axis=1).astype(BF16), vbands[u]))
        outs = []
        for u in range(nb):
            acc_new = jnp.where(left, res[u][:blk], res[u][blk:])
            l_new = jnp.where(left, sums[u][:blk], sums[u][blk:])
            m0_new, m1_new = m_new[u][:blk], m_new[u][blk:]
            if states[u] is not None:
                alpha = jnp.where(left, jnp.exp2(states[u][0] - m0_new), jnp.exp2(states[u][1] - m1_new))
                acc_new = alpha * states[u][2] + acc_new
                l_new = alpha * states[u][3] + l_new
            outs.append((m0_new, m1_new, acc_new, l_new))
        return outs

    for bi, (window, dil) in enumerate(ATT_ORDER):
        assert window // dil == blk
        q_ref, k_ref, v_ref = qkv_refs[3 * bi:3 * bi + 3]
        nblk = seq // dil // blk
        first = bi == 0
        last = bi == nbr - 1
        unroll = min(ATT_UNROLL_MAX, nblk)
        assert nblk % unroll == 0 and (not last or dil == 1)
        for r in range(dil):
            col = r * LANES

            def body(it, carry, first=first, last=last, dil=dil, r=r, col=col, unroll=unroll,
                     q_ref=q_ref, k_ref=k_ref, v_ref=v_ref):
                n0 = it * unroll

                def tile(ref, n):
                    if isinstance(n, int):
                        return ref[0, 0, n * blk:(n + 1) * blk, col:col + LANES]
                    return ref[0, 0, pl.ds(pl.multiple_of(n * blk, blk), blk), col:col + LANES]

                def state_rows(n):
                    if dil == 1:
                        return pl.ds(n * blk if isinstance(n, int) else pl.multiple_of(n * blk, blk), blk)
                    return pl.ds(n * (blk * dil) + r, blk, stride=dil)

                n_prev = max(n0 - 1, 0) if isinstance(n0, int) else jnp.maximum(n0 - 1, 0)
                ks = [tile(k_ref, n_prev)] + [tile(k_ref, n0 + u) for u in range(unroll)]
                vs = [tile(v_ref, n_prev)] + [tile(v_ref, n0 + u) for u in range(unroll)]
                qs = [tile(q_ref, n0 + u) for u in range(unroll)]
                rows = [state_rows(n0 + u) for u in range(unroll)]
                states = [None if first else (m0_ref[rw, :], m1_ref[rw, :], acc_ref[rw, :], l_ref[rw, :])
                          for rw in rows]
                if isinstance(n0, int):
                    pen0 = None if n0 > 0 else neg
                else:
                    pen0 = jnp.where(n0 > 0, 0.0, neg).astype(F32)
                kbands = [jnp.concatenate([ks[u], ks[u + 1]], axis=0) for u in range(unroll)]
                vbands = [jnp.concatenate([vs[u], vs[u + 1]], axis=0) for u in range(unroll)]
                outs = softmax_blocks(qs, kbands, vbands, [pen0] + [None] * (unroll - 1), states)
                for rw, (m0_new, m1_new, acc_new, l_new) in zip(rows, outs):
                    if last:
                        o_ref[0, 0, rw, :] = (acc_new / l_new).astype(o_ref.dtype)
                    else:
                        m0_ref[rw, :] = m0_new
                        m1_ref[rw, :] = m1_new
                        acc_ref[rw, :] = acc_new
                        l_ref[rw, :] = l_new
                return carry

            if nblk == unroll:
                body(0, 0)
            else:
                lax.fori_loop(0, nblk // unroll, body, 0)


def _dilated_attention(views):
    pairs, b = views[-1].shape[:2]
    s = views[-1].shape[2] * ATT_ORDER[-1][1]
    specs = [pl.BlockSpec((1, 1) + t.shape[2:], lambda i, j: (j, i, 0, 0)) for t in views]
    return pl.pallas_call(
        functools.partial(_attn_kernel, s),
        out_shape=jax.ShapeDtypeStruct((pairs, b, s, LANES), BF16),
        grid=(b, pairs),
        in_specs=specs,
        out_specs=pl.BlockSpec((1, 1, s, LANES), lambda i, j: (j, i, 0, 0)),
        scratch_shapes=[pltpu.VMEM((s, LANES), F32)] * 4,
        compiler_params=_params("arbitrary", "arbitrary"),
        name="dilated_attention",
    )(*views)


FFN_COLS = 256
OUT_COLS = 256


def _mix_ffn_kernel(cfg, x_ref, mod_ref, mix_ref, att_ref, wout_ref, g_ref, up_ref, cw_ref, cb_ref, down_ref,
                    fg_ref, o_ref, x1, carry, act):
    ffn, final_norm = cfg
    tm, d = x_ref.shape[1:]
    fc = FFN_COLS
    halo = SUBLANES
    mw = mix_ref.shape[-1]

    for c0 in range(0, d, OUT_COLS):
        y = _dot(mix_ref[0], wout_ref[0:mw, c0:c0 + OUT_COLS])
        for p in range(att_ref.shape[0]):
            y = y + _dot(att_ref[p, 0], wout_ref[mw + p * LANES:mw + (p + 1) * LANES, c0:c0 + OUT_COLS])
        x1[:, c0:c0 + OUT_COLS] = x_ref[0, :, c0:c0 + OUT_COLS] + mod_ref[0, 2:3, c0:c0 + OUT_COLS] * y

    h = _modulated_norm(x1[...], g_ref[...], mod_ref[0, 3:4, :], mod_ref[0, 4:5, :]).astype(BF16)

    @pl.when(pl.program_id(1) == 0)
    def _():
        carry[...] = jnp.zeros(carry.shape, F32)

    def conv_act(c0):
        hid = _dot(h, up_ref[:, c0:c0 + fc])
        prev = carry[:, c0:c0 + fc]
        carry[:, c0:c0 + fc] = hid[tm - halo:tm, :]
        return _causal_dwconv(hid, prev, cw_ref, cb_ref, c0, FFN_CONV_K)

    for j in range(ffn // fc):
        gate = conv_act(j * fc)
        val = conv_act(ffn + j * fc)
        act[:, j * fc:(j + 1) * fc] = (_silu(gate) * val).astype(BF16)

    for c0 in range(0, d, OUT_COLS):
        y = _dot(act[...], down_ref[:, c0:c0 + OUT_COLS])
        o_ref[0, :, c0:c0 + OUT_COLS] = x1[:, c0:c0 + OUT_COLS] + mod_ref[0, 5:6, c0:c0 + OUT_COLS] * y
    if final_norm:
        y = o_ref[0]
        o_ref[0] = y * lax.rsqrt(jnp.mean(y * y, axis=-1, keepdims=True) + NORM_EPS) * fg_ref[...]


def _mix_ffn(x, mod, mix, att, w_out, gain, up, cw, cb, down, final_g, final_norm, tm):
    b, s, d = x.shape
    ffn = down.shape[0]
    pairs = att.shape[0]
    assert ffn % FFN_COLS == 0 and d % OUT_COLS == 0
    row = lambda i, j: (i, j, 0)
    return pl.pallas_call(
        functools.partial(_mix_ffn_kernel, (ffn, final_norm)),
        out_shape=jax.ShapeDtypeStruct((b, s, d), F32),
        grid=(b, s // tm),
        in_specs=[
            pl.BlockSpec((1, tm, d), row),
            pl.BlockSpec((1, SUBLANES, d), lambda i, j: (i, 0, 0)),
            pl.BlockSpec((1, tm, mix.shape[-1]), row),
            pl.BlockSpec((pairs, 1, tm, LANES), lambda i, j: (0, i, j, 0)),
            _resident(w_out), _resident(gain), _resident(up), _resident(cw), _resident(cb), _resident(down),
            _resident(final_g),
        ],
        out_specs=pl.BlockSpec((1, tm, d), row),
        scratch_shapes=[
            pltpu.VMEM((tm, d), F32),
            pltpu.VMEM((SUBLANES, 2 * ffn), F32),
            pltpu.VMEM((tm, ffn), BF16),
        ],
        compiler_params=_params("arbitrary", "arbitrary"),
        name="mix_ffn",
    )(x, mod, mix, att, w_out, gain, up, cw, cb, down, final_g)


def _tile_rows(s):
    pick = lambda top: max(t for t in (256, 512, 1024) if t <= top and s % t == 0)
    return pick(1024), pick(512), pick(1024)


def _pad_lanes(v, width=LANES):
    return jnp.pad(v, ((0, 0), (0, width - v.shape[-1])))


def kernel(x, c, positions, ada_w, ada_b, norm1_g, w_in, ssd_conv_w, ssd_conv_b, ssd_dt_bias, ssd_a_log,
           ssd_d, ssd_norm_g, pool_w, pool_scale, w_out, norm2_g, ffn_up, ffn_conv_w, ffn_conv_b, ffn_down,
           final_g):
    depth = w_in.shape[0]
    b, s, d = x.shape
    heads = ssd_dt_bias.shape[-1]
    inner = heads * SSD_HEAD_DIM
    conv_ch = ssd_conv_w.shape[-1]
    pool_width = pool_scale.shape[-1]
    ssd_proj = inner + conv_ch + heads
    att_w = (w_in.shape[-1] - ssd_proj - pool_width) // 3
    widths = (inner, conv_ch, pool_width, att_w)
    assert conv_ch == inner + 2 * SSD_GROUPS * SSD_STATE and att_w % LANES == 0
    assert s % (ATT_BLOCK * ATT_PATTERNS[-1][1]) == 0 and s % SSD_CHUNK == 0
    tm_in, tm_ffn, ssd_rows = _tile_rows(s)

    mod_all = _ada_mod(c, ada_w, ada_b).reshape(depth, b, 6, d)
    mod_all = jnp.pad(mod_all, ((0, 0), (0, 0), (0, SUBLANES - 6), (0, 0)))
    cos, sin = _rope_tables(positions)
    a0 = ssd_proj

    for i in range(depth):
        mod = mod_all[i]
        wi = w_in[i]
        a1 = a0 + pool_width
        qk_cols = [wi[:, a1 + t * att_w + p * LANES:a1 + t * att_w + (p + 1) * LANES]
                   for p in range(att_w // LANES) for t in range(2)]
        w_main = jnp.concatenate([wi[:, :inner + conv_ch], wi[:, a0:a1], *qk_cols, wi[:, a1 + 2 * att_w:]],
                                 axis=1).astype(BF16)
        w_dt = _pad_lanes(wi[:, inner + conv_ch:a0]).astype(BF16)
        z, xbc, dt, u, *views = _in_projection(x, mod, norm1_g[i][None], w_main, w_dt, cos, sin, ssd_conv_w[i],
                                               ssd_conv_b[i][None], widths, tm_in)
        pw_bd = jax.scipy.linalg.block_diag(*[pool_w[i, g] for g in range(pool_w.shape[1])]).astype(BF16)
        mix = _ssd_pool(
            z, xbc, dt, u, _pad_lanes(ssd_dt_bias[i][None]),
            _pad_lanes(ssd_a_log[i][None]), jnp.repeat(ssd_d[i], SSD_HEAD_DIM)[None], ssd_norm_g[i][None],
            pw_bd, pool_scale[i][None], heads, ssd_rows)
        att = _dilated_attention(views)
        x = _mix_ffn(x, mod, mix, att, w_out[i].astype(BF16), norm2_g[i][None], ffn_up[i].astype(BF16),
                     ffn_conv_w[i], ffn_conv_b[i][None], ffn_down[i].astype(BF16), final_g[None],
                     i == depth - 1, tm_ffn)
    return x
```

```python
import functools

import jax
import jax.numpy as jnp
from jax import lax
from jax.experimental import pallas as pl
from jax.experimental.pallas import tpu as pltpu

F32 = jnp.float32
BF16 = jnp.bfloat16

LANES = 128
SUBLANES = 8
VMEM_LIMIT = 56 * 1024 * 1024

NORM_EPS = 1e-6
ROPE_THETA = 500000.0
LOG2_E = 1.4426950408889634

SSD_HEAD_DIM = 64
SSD_GROUPS = 2
SSD_STATE = 128
SSD_CONV_K = 4
SSD_CHUNK = 256
POOL_WINDOWS = (2, 4, 8, 16)
ATT_HEAD_DIM = 64
ATT_PATTERNS = ((128, 1), (512, 4), (2048, 16))
ATT_BLOCK = 128
ATT_ORDER = tuple(sorted(ATT_PATTERNS, key=lambda wd: -wd[1]))
ATT_UNROLL_MAX = 4
ROT_DIM = ATT_HEAD_DIM // 4
FFN_CONV_K = 3

CONV_HALO = SUBLANES
POOL_HALO = 2 * SUBLANES


def _params(*semantics):
    return pltpu.CompilerParams(dimension_semantics=semantics, vmem_limit_bytes=VMEM_LIMIT)


def _resident(a):
    return pl.BlockSpec(a.shape, lambda *_: (0,) * a.ndim, pipeline_mode=pl.Buffered(1))


def _silu(v):
    return v * jax.nn.sigmoid(v)


def _dot(a, b):
    return jnp.dot(a, b, preferred_element_type=F32)


def _dot_nt(a, b):
    return lax.dot_general(a, b, (((1,), (1,)), ((), ())), preferred_element_type=F32)


def _ada_kernel(c_ref, w_ref, b_ref, o_ref):
    c_act = _silu(c_ref[...])
    o_ref[0] = jnp.dot(c_act, w_ref[0], preferred_element_type=F32,
                       precision=lax.Precision.HIGHEST) + b_ref[0]


def _ada_mod(c, ada_w, ada_b):
    depth, d, n = ada_w.shape
    b = c.shape[0]
    tn = d
    return pl.pallas_call(
        _ada_kernel,
        out_shape=jax.ShapeDtypeStruct((depth, b, n), F32),
        grid=(depth, n // tn),
        in_specs=[
            pl.BlockSpec((b, d), lambda i, j: (0, 0)),
            pl.BlockSpec((1, d, tn), lambda i, j: (i, 0, j)),
            pl.BlockSpec((1, 1, tn), lambda i, j: (i, 0, j)),
        ],
        out_specs=pl.BlockSpec((1, b, tn), lambda i, j: (i, 0, j)),
        compiler_params=_params("arbitrary", "arbitrary"),
        name="ada_mod",
    )(c, ada_w, ada_b.reshape(depth, 1, n))


def _rope_kernel(pos_ref, freq_ref, ecos_ref, esin_ref, cos_ref, sin_ref):
    ang = freq_ref[...] * pos_ref[0]
    rows = [jnp.cos(ang), jnp.sin(ang), jnp.ones_like(ang)]
    rows.append(jnp.zeros((LANES - len(rows) * ang.shape[0], ang.shape[1]), F32))
    t = jnp.concatenate(rows, axis=0).T
    cos_ref[0] = jnp.dot(t, ecos_ref[...], preferred_element_type=F32, precision=lax.Precision.HIGHEST)
    sin_ref[0] = jnp.dot(t, esin_ref[...], preferred_element_type=F32, precision=lax.Precision.HIGHEST)


def _rope_tables(positions):
    b, s = positions.shape
    half = ROT_DIM // 2
    assert half == SUBLANES
    inv_freq = ROPE_THETA ** (-jnp.arange(0, ROT_DIM, 2, dtype=F32) / ROT_DIM)
    lane = jnp.arange(LANES)[None, :] % ATT_HEAD_DIM
    row = jnp.arange(LANES)[:, None]
    rotary = lane < ROT_DIM
    ecos = jnp.where(rotary, row == lane % half, row == 2 * half).astype(F32)
    sign = jnp.where(lane < half, -1.0, 1.0)
    esin = jnp.where(rotary & (row == half + lane % half), sign, 0.0).astype(F32)
    return pl.pallas_call(
        _rope_kernel,
        out_shape=(jax.ShapeDtypeStruct((b, s, LANES), F32),) * 2,
        grid=(b,),
        in_specs=[
            pl.BlockSpec((1, 1, s), lambda i: (i, 0, 0)),
            pl.BlockSpec((half, 1), lambda i: (0, 0)),
            pl.BlockSpec((LANES, LANES), lambda i: (0, 0)),
            pl.BlockSpec((LANES, LANES), lambda i: (0, 0)),
        ],
        out_specs=(pl.BlockSpec((1, s, LANES), lambda i: (i, 0, 0)),) * 2,
        compiler_params=_params("arbitrary"),
        name="rope_tables",
    )(positions.astype(F32)[:, None, :], inv_freq[:, None], ecos, esin)


def _modulated_norm(x, gain, shift, scale):
    y = x * lax.rsqrt(jnp.mean(x * x, axis=-1, keepdims=True) + NORM_EPS)
    return (y * gain) * (1.0 + scale) + shift


def _rotate(t, cos, sin):
    half = ROT_DIM // 2
    lane = lax.broadcasted_iota(jnp.int32, t.shape, 1) % ATT_HEAD_DIM
    partner = jnp.where(lane < half, pltpu.roll(t, LANES - half, 1), pltpu.roll(t, half, 1))
    return t * cos + partner * sin


def _inproj_kernel(widths, x_ref, mod_ref, g_ref, w_ref, wdt_ref, cos_ref, sin_ref,
                   z_ref, xbc_ref, dt_ref, u_ref, *rest):
    qkv_refs, slabs = rest[:-1], rest[-1]
    inner, conv_ch, pool_w, att_w = widths
    tm = x_ref.shape[1]
    h = _modulated_norm(x_ref[0], g_ref[...], mod_ref[0, 0:1, :], mod_ref[0, 1:2, :]).astype(BF16)
    o = 0
    z_ref[0] = _silu(_dot(h, w_ref[:, o:o + inner])).astype(z_ref.dtype)
    o += inner
    xbc_ref[0] = _dot(h, w_ref[:, o:o + conv_ch]).astype(xbc_ref.dtype)
    o += conv_ch
    u_ref[0] = _dot(h, w_ref[:, o:o + pool_w]).astype(u_ref.dtype)
    o += pool_w
    dt_ref[0] = _dot(h, wdt_ref[...])
    cos, sin = cos_ref[0], sin_ref[0]
    scale = ATT_HEAD_DIM ** -0.5 * LOG2_E
    pairs = att_w // LANES

    def emit(t, which, p):
        slab = slabs.at[which * pairs + p]
        slab[...] = t
        for bi, (_, dil) in enumerate(ATT_ORDER):
            ref = qkv_refs[3 * bi + which]
            if dil == 1:
                ref[p, 0] = t.astype(BF16)
            else:
                for r in range(dil):
                    ref[p, 0, :, r * LANES:(r + 1) * LANES] = (
                        slab[pl.ds(r, tm // dil, stride=dil), :].astype(BF16))

    for p in range(pairs):
        qk = _dot(h, w_ref[:, o + 2 * p * LANES:o + 2 * (p + 1) * LANES])
        emit(_rotate(qk[:, :LANES], cos, sin) * scale, 0, p)
        emit(_rotate(qk[:, LANES:], cos, sin), 1, p)
    v = _dot(h, w_ref[:, o + 2 * att_w:o + 3 * att_w])
    for p in range(pairs):
        emit(v[:, p * LANES:(p + 1) * LANES], 2, p)


def _in_projection(x, mod, gain, w_main, w_dt, cos, sin, widths, tm):
    b, s, d = x.shape
    inner, conv_ch, pool_w, att_w = widths
    pairs = att_w // LANES
    row = lambda i, j: (i, j, 0)
    view_shapes, view_specs = [], []
    for _, dil in ATT_ORDER:
        assert tm % (dil * 2 * SUBLANES) == 0
        view_shapes += [jax.ShapeDtypeStruct((pairs, b, s // dil, dil * LANES), BF16)] * 3
        view_specs += [pl.BlockSpec((pairs, 1, tm // dil, dil * LANES), lambda i, j: (0, i, j, 0))] * 3
    return pl.pallas_call(
        functools.partial(_inproj_kernel, widths),
        out_shape=(
            jax.ShapeDtypeStruct((b, s, inner), BF16),
            jax.ShapeDtypeStruct((b, s, conv_ch), BF16),
            jax.ShapeDtypeStruct((b, s, LANES), F32),
            jax.ShapeDtypeStruct((b, s, pool_w), BF16),
            *view_shapes,
        ),
        grid=(b, s // tm),
        in_specs=[
            pl.BlockSpec((1, tm, d), row),
            pl.BlockSpec((1, SUBLANES, d), lambda i, j: (i, 0, 0)),
            _resident(gain), _resident(w_main), _resident(w_dt),
            pl.BlockSpec((1, tm, LANES), row),
            pl.BlockSpec((1, tm, LANES), row),
        ],
        out_specs=(
            pl.BlockSpec((1, tm, inner), row),
            pl.BlockSpec((1, tm, conv_ch), row),
            pl.BlockSpec((1, tm, LANES), row),
            pl.BlockSpec((1, tm, pool_w), row),
            *view_specs,
        ),
        scratch_shapes=[pltpu.VMEM((3 * pairs, tm, LANES), F32)],
        compiler_params=_params("arbitrary", "arbitrary"),
        name="in_projection",
    )(x, mod, gain, w_main, w_dt, cos, sin)


def _pair_rep(col, h0):
    lane = lax.broadcasted_iota(jnp.int32, (col.shape[0], LANES), 1)
    return jnp.where(lane < SSD_HEAD_DIM, col[:, h0:h0 + 1], col[:, h0 + 1:h0 + 2])


def _ssd_pool_kernel(dims, z_ref, xbc_ref, dt_ref, u_ref, cw_ref, cb_ref, dtb_ref, alog_ref, dskip_ref,
                     ng_ref, pw_ref, ps_ref, o_ref, xbuf, ubuf, hstate):
    inner, heads = dims
    q = SSD_CHUNK
    blk_rows = z_ref.shape[1]
    gw = inner // SSD_GROUPS
    n = SSD_STATE
    step = pl.program_id(1)

    @pl.when(step == 0)
    def _():
        xbuf[0:CONV_HALO, :] = jnp.zeros((CONV_HALO, xbuf.shape[1]), F32)
        ubuf[0:POOL_HALO, :] = jnp.zeros((POOL_HALO, ubuf.shape[1]), F32)
        hstate[...] = jnp.zeros(hstate.shape, F32)

    xbuf[CONV_HALO:CONV_HALO + blk_rows, :] = xbc_ref[0].astype(F32)
    ext = xbuf[...]
    conv = cb_ref[...] + cw_ref[SSD_CONV_K - 1:SSD_CONV_K, :] * ext
    for k in range(1, SSD_CONV_K):
        conv = conv + cw_ref[SSD_CONV_K - 1 - k:SSD_CONV_K - k, :] * pltpu.roll(ext, k, 0)
    xbc_all = _silu(conv[CONV_HALO:, :])
    xbuf[0:CONV_HALO, :] = ext[blk_rows:blk_rows + CONV_HALO, :]

    lane = lax.broadcasted_iota(jnp.int32, (blk_rows, LANES), 1)
    dt_all = jax.nn.softplus(dt_ref[0] + dtb_ref[...])
    a = -jnp.exp(alog_ref[...])
    da_all = jnp.where(lane < heads, dt_all * a, 0.0)
    rows = lax.broadcasted_iota(jnp.int32, (q, q), 0)
    cols = lax.broadcasted_iota(jnp.int32, (q, q), 1)
    causal = rows >= cols
    tri = jnp.where(causal, 1.0, 0.0).astype(F32)
    left = lax.broadcasted_iota(jnp.int32, (q, LANES), 1) < SSD_HEAD_DIM
    hpg = heads // SSD_GROUPS

    for ci in range(blk_rows // q):
        r0 = ci * q
        xbc = xbc_all[r0:r0 + q, :]
        dt = dt_all[r0:r0 + q, :]
        a_col = jnp.dot(tri, da_all[r0:r0 + q, :], preferred_element_type=F32,
                        precision=lax.Precision.HIGHEST)
        a_row = a_col.T
        y_groups = []
        for g in range(SSD_GROUPS):
            xs = xbc[:, g * gw:(g + 1) * gw]
            bm_f = xbc[:, inner + g * n:inner + (g + 1) * n]
            bm = bm_f.astype(BF16)
            bm_t = bm_f.T.astype(BF16)
            cm = xbc[:, inner + SSD_GROUPS * n + g * n:inner + SSD_GROUPS * n + (g + 1) * n].astype(BF16)
            cb = _dot_nt(cm, bm)
            y_off = _dot(cm, hstate[g].astype(BF16))
            y_pairs, xd_pairs, cd_pairs = [], [], []
            for p in range(hpg // 2):
                h0 = g * hpg + 2 * p
                xs_p = xs[:, p * LANES:(p + 1) * LANES]
                acum = _pair_rep(a_col, h0)
                xdt = xs_p * _pair_rep(dt, h0)
                y_p = y_off[:, p * LANES:(p + 1) * LANES] * jnp.exp(acum)
                for side in range(2):
                    hh = h0 + side
                    seg = a_col[:, hh:hh + 1] - a_row[hh:hh + 1, :]
                    lmat = jnp.exp(jnp.where(causal, seg, -jnp.inf))
                    keep = left if side == 0 else jnp.logical_not(left)
                    y_p = y_p + _dot((cb * lmat).astype(BF16), jnp.where(keep, xdt, 0.0).astype(BF16))
                a_last = acum[q - 1:q, :]
                xd_pairs.append((xdt * jnp.exp(a_last - acum)).astype(BF16))
                cd_pairs.append(jnp.exp(a_last))
                y_pairs.append(y_p + dskip_ref[:, h0 * SSD_HEAD_DIM:(h0 + 2) * SSD_HEAD_DIM] * xs_p)
            xd = jnp.concatenate(xd_pairs, axis=1)
            hstate[g] = hstate[g] * jnp.concatenate(cd_pairs, axis=1) + _dot(bm_t, xd)
            y = jnp.concatenate(y_pairs, axis=1) * z_ref[0, r0:r0 + q, g * gw:(g + 1) * gw].astype(F32)
            y = y * lax.rsqrt(jnp.mean(y * y, axis=-1, keepdims=True) + NORM_EPS)
            y_groups.append(y * ng_ref[:, g * gw:(g + 1) * gw])
        o_ref[0, r0:r0 + q, 0:inner] = jnp.concatenate(y_groups, axis=1).astype(o_ref.dtype)

    pool_w = ubuf.shape[1]
    pool_ch = pool_w // len(POOL_WINDOWS)
    u = u_ref[0].astype(F32)
    ubuf[POOL_HALO:POOL_HALO + blk_rows, :] = u
    ext = ubuf[...]
    plane = lax.broadcasted_iota(jnp.int32, (blk_rows, pool_w), 1)
    pos = step * blk_rows + lax.broadcasted_iota(jnp.int32, (blk_rows, pool_w), 0)
    pooled = jnp.zeros((blk_rows, pool_w), F32)
    win = jnp.zeros((blk_rows, pool_w), jnp.int32)
    acc, width = ext, 1
    for gi, w in enumerate(POOL_WINDOWS):
        while width < w:
            acc = acc + pltpu.roll(acc, width, 0)
            width *= 2
        assert width == w
        in_group = (plane >= gi * pool_ch) & (plane < (gi + 1) * pool_ch)
        pooled = jnp.where(in_group, acc[POOL_HALO:, :], pooled)
        win = jnp.where(in_group, w, win)
    cnt = jnp.minimum(pos + 1, win).astype(F32)
    diff = (pooled / cnt - u).astype(BF16)
    ubuf[0:POOL_HALO, :] = ext[blk_rows:blk_rows + POOL_HALO, :]
    o_ref[0, :, inner:inner + pool_w] = (_dot(diff, pw_ref[...]) * ps_ref[...]).astype(o_ref.dtype)


def _ssd_pool(z, xbc, dt, u, cw, cb, dtb, alog, dskip, ng, pw, ps, heads, blk_rows):
    b, s, inner = z.shape
    conv_ch = xbc.shape[-1]
    pool_w = u.shape[-1]
    assert blk_rows % SSD_CHUNK == 0 and s % blk_rows == 0
    row = lambda i, j: (i, j, 0)
    full = _resident
    out_w = inner + pool_w
    return pl.pallas_call(
        functools.partial(_ssd_pool_kernel, (inner, heads)),
        out_shape=jax.ShapeDtypeStruct((b, s, out_w), BF16),
        grid=(b, s // blk_rows),
        in_specs=[
            pl.BlockSpec((1, blk_rows, inner), row),
            pl.BlockSpec((1, blk_rows, conv_ch), row),
            pl.BlockSpec((1, blk_rows, LANES), row),
            pl.BlockSpec((1, blk_rows, pool_w), row),
            full(cw), full(cb), full(dtb), full(alog), full(dskip), full(ng), full(pw), full(ps),
        ],
        out_specs=pl.BlockSpec((1, blk_rows, out_w), row),
        scratch_shapes=[
            pltpu.VMEM((CONV_HALO + blk_rows, conv_ch), F32),
            pltpu.VMEM((POOL_HALO + blk_rows, pool_w), F32),
            pltpu.VMEM((SSD_GROUPS, SSD_STATE, inner // SSD_GROUPS), F32),
        ],
        compiler_params=_params("arbitrary", "arbitrary"),
        name="ssd_pool",
    )(z, xbc, dt, u, cw, cb, dtb, alog, dskip, ng, pw, ps)


def _attn_kernel(seq, *refs):
    nbr = len(ATT_ORDER)
    qkv_refs = refs[:3 * nbr]
    o_ref = refs[3 * nbr]
    acc_ref, l_ref, m0_ref, m1_ref = refs[3 * nbr + 1:]
    blk = ATT_BLOCK
    hd = ATT_HEAD_DIM
    lane = lax.broadcasted_iota(jnp.int32, (blk, LANES), 1)
    left = lane < hd
    qi = lax.broadcasted_iota(jnp.int32, (blk, blk), 0)
    kj = lax.broadcasted_iota(jnp.int32, (blk, blk), 1)
    mask_cur = kj <= qi
    mask_prev = kj >= qi
    neg = -jnp.inf

    mask_prev2 = jnp.concatenate([mask_prev, mask_prev], axis=0)
    mask_cur2 = jnp.concatenate([mask_cur, mask_cur], axis=0)

    def softmax_blocks(qs, kbands, vbands, pens, states):
        nb = len(qs)
        zero = jnp.zeros_like(qs[0])
        scores = []
        for u in range(nb):
            q2 = jnp.concatenate([jnp.where(left, qs[u], zero), jnp.where(left, zero, qs[u])], axis=0)
            s = _dot_nt(q2, kbands[u])
            s_p = s[:, :blk] if pens[u] is None else s[:, :blk] + pens[u]
            scores.append((jnp.where(mask_prev2, s_p, neg), jnp.where(mask_cur2, s[:, blk:], neg)))
        m_new = []
        for u in range(nb):
            s_p, s_c = scores[u]
            m_blk = jnp.max(jnp.maximum(s_p, s_c), axis=-1, keepdims=True)
            if states[u] is None:
                m_new.append(jnp.broadcast_to(m_blk, (2 * blk, LANES)))
            else:
                m_new.append(jnp.maximum(jnp.concatenate(states[u][:2], axis=0), m_blk))
        res, sums = [], []
        for u in range(nb):
            s_p, s_c = scores[u]
            p_p, p_c = jnp.exp2(s_p - m_new[u]), jnp.exp2(s_c - m_new[u])
            sums.append(jnp.sum(p_p + p_c, axis=-1, keepdims=True))
            res.append(_dot(jnp.concatenate([p_p, p_c], axis=1).astype(BF16), vbands[u]))
        outs = []
        for u in range(nb):
            acc_new = jnp.where(left, res[u][:blk], res[u][blk:])
            l_new = jnp.where(left, sums[u][:blk], sums[u][blk:])
            m0_new, m1_new = m_new[u][:blk], m_new[u][blk:]
            if states[u] is not None:
                alpha = jnp.where(left, jnp.exp2(states[u][0] - m0_new), jnp.exp2(states[u][1] - m1_new))
                acc_new = alpha * states[u][2] + acc_new
                l_new = alpha * states[u][3] + l_new
            outs.append((m0_new, m1_new, acc_new, l_new))
        return outs

    for bi, (window, dil) in enumerate(ATT_ORDER):
        assert window // dil == blk
        q_ref, k_ref, v_ref = qkv_refs[3 * bi:3 * bi + 3]
        nblk = seq // dil // blk
        first = bi == 0
        last = bi == nbr - 1
        unroll = min(ATT_UNROLL_MAX, nblk)
        assert nblk % unroll == 0 and (not last or dil == 1)
        for r in range(dil):
            col = r * LANES

            def body(it, carry, first=first, last=last, dil=dil, r=r, col=col, unroll=unroll,
                     q_ref=q_ref, k_ref=k_ref, v_ref=v_ref):
                n0 = it * unroll

                def tile(ref, n):
                    if isinstance(n, int):
                        return ref[0, 0, n * blk:(n + 1) * blk, col:col + LANES]
                    return ref[0, 0, pl.ds(pl.multiple_of(n * blk, blk), blk), col:col + LANES]

                def state_rows(n):
                    if dil == 1:
                        return pl.ds(n * blk if isinstance(n, int) else pl.multiple_of(n * blk, blk), blk)
                    return pl.ds(n * (blk * dil) + r, blk, stride=dil)

                n_prev = max(n0 - 1, 0) if isinstance(n0, int) else jnp.maximum(n0 - 1, 0)
                ks = [tile(k_ref, n_prev)] + [tile(k_ref, n0 + u) for u in range(unroll)]
                vs = [tile(v_ref, n_prev)] + [tile(v_ref, n0 + u) for u in range(unroll)]
                qs = [tile(q_ref, n0 + u) for u in range(unroll)]
                rows = [state_rows(n0 + u) for u in range(unroll)]
                states = [None if first else (m0_ref[rw, :], m1_ref[rw, :], acc_ref[rw, :], l_ref[rw, :])
                          for rw in rows]
                if isinstance(n0, int):
                    pen0 = None if n0 > 0 else neg
                else:
                    pen0 = jnp.where(n0 > 0, 0.0, neg).astype(F32)
                kbands = [jnp.concatenate([ks[u], ks[u + 1]], axis=0) for u in range(unroll)]
                vbands = [jnp.concatenate([vs[u], vs[u + 1]], axis=0) for u in range(unroll)]
                outs = softmax_blocks(qs, kbands, vbands, [pen0] + [None] * (unroll - 1), states)
                for rw, (m0_new, m1_new, acc_new, l_new) in zip(rows, outs):
                    if last:
                        o_ref[0, 0, rw, :] = (acc_new / l_new).astype(o_ref.dtype)
                    else:
                        m0_ref[rw, :] = m0_new
                        m1_ref[rw, :] = m1_new
                        acc_ref[rw, :] = acc_new
                        l_ref[rw, :] = l_new
                return carry

            if nblk == unroll:
                body(0, 0)
            else:
                lax.fori_loop(0, nblk // unroll, body, 0)


def _dilated_attention(views):
    pairs, b = views[-1].shape[:2]
    s = views[-1].shape[2] * ATT_ORDER[-1][1]
    specs = [pl.BlockSpec((1, 1) + t.shape[2:], lambda i, j: (j, i, 0, 0)) for t in views]
    return pl.pallas_call(
        functools.partial(_attn_kernel, s),
        out_shape=jax.ShapeDtypeStruct((pairs, b, s, LANES), BF16),
        grid=(b, pairs),
        in_specs=specs,
        out_specs=pl.BlockSpec((1, 1, s, LANES), lambda i, j: (j, i, 0, 0)),
        scratch_shapes=[pltpu.VMEM((s, LANES), F32)] * 4,
        compiler_params=_params("arbitrary", "arbitrary"),
        name="dilated_attention",
    )(*views)


FFN_COLS = 256
OUT_COLS = 256


def _mix_ffn_kernel(cfg, x_ref, mod_ref, mix_ref, att_ref, wout_ref, g_ref, up_ref, cw_ref, cb_ref, down_ref,
                    fg_ref, o_ref, x1, carry, act):
    ffn, final_norm = cfg
    tm, d = x_ref.shape[1:]
    fc = FFN_COLS
    halo = SUBLANES
    mw = mix_ref.shape[-1]

    for c0 in range(0, d, OUT_COLS):
        y = _dot(mix_ref[0], wout_ref[0:mw, c0:c0 + OUT_COLS])
        for p in range(att_ref.shape[0]):
            y = y + _dot(att_ref[p, 0], wout_ref[mw + p * LANES:mw + (p + 1) * LANES, c0:c0 + OUT_COLS])
        x1[:, c0:c0 + OUT_COLS] = x_ref[0, :, c0:c0 + OUT_COLS] + mod_ref[0, 2:3, c0:c0 + OUT_COLS] * y

    h = _modulated_norm(x1[...], g_ref[...], mod_ref[0, 3:4, :], mod_ref[0, 4:5, :]).astype(BF16)

    @pl.when(pl.program_id(1) == 0)
    def _():
        carry[...] = jnp.zeros(carry.shape, F32)

    top_row = lax.broadcasted_iota(jnp.int32, (halo, fc), 0)

    def conv_act(c0):
        hid = _dot(h, up_ref[:, c0:c0 + fc])
        prev = carry[:, c0:c0 + fc]
        carry[:, c0:c0 + fc] = hid[tm - halo:tm, :]
        out = cb_ref[:, c0:c0 + fc] + cw_ref[FFN_CONV_K - 1:FFN_CONV_K, c0:c0 + fc] * hid
        for k in range(1, FFN_CONV_K):
            sh = pltpu.roll(hid, k, 0)
            top = jnp.where(top_row < k, pltpu.roll(prev, k, 0), sh[0:halo, :])
            sh = jnp.concatenate([top, sh[halo:, :]], axis=0)
            out = out + cw_ref[FFN_CONV_K - 1 - k:FFN_CONV_K - k, c0:c0 + fc] * sh
        return out

    for j in range(ffn // fc):
        gate = conv_act(j * fc)
        val = conv_act(ffn + j * fc)
        act[:, j * fc:(j + 1) * fc] = (_silu(gate) * val).astype(BF16)

    for c0 in range(0, d, OUT_COLS):
        y = _dot(act[...], down_ref[:, c0:c0 + OUT_COLS])
        o_ref[0, :, c0:c0 + OUT_COLS] = x1[:, c0:c0 + OUT_COLS] + mod_ref[0, 5:6, c0:c0 + OUT_COLS] * y
    if final_norm:
        y = o_ref[0]
        o_ref[0] = y * lax.rsqrt(jnp.mean(y * y, axis=-1, keepdims=True) + NORM_EPS) * fg_ref[...]


def _mix_ffn(x, mod, mix, att, w_out, gain, up, cw, cb, down, final_g, final_norm, tm):
    b, s, d = x.shape
    ffn = down.shape[0]
    pairs = att.shape[0]
    assert ffn % FFN_COLS == 0 and d % OUT_COLS == 0
    row = lambda i, j: (i, j, 0)
    return pl.pallas_call(
        functools.partial(_mix_ffn_kernel, (ffn, final_norm)),
        out_shape=jax.ShapeDtypeStruct((b, s, d), F32),
        grid=(b, s // tm),
        in_specs=[
            pl.BlockSpec((1, tm, d), row),
            pl.BlockSpec((1, SUBLANES, d), lambda i, j: (i, 0, 0)),
            pl.BlockSpec((1, tm, mix.shape[-1]), row),
            pl.BlockSpec((pairs, 1, tm, LANES), lambda i, j: (0, i, j, 0)),
            _resident(w_out), _resident(gain), _resident(up), _resident(cw), _resident(cb), _resident(down),
            _resident(final_g),
        ],
        out_specs=pl.BlockSpec((1, tm, d), row),
        scratch_shapes=[
            pltpu.VMEM((tm, d), F32),
            pltpu.VMEM((SUBLANES, 2 * ffn), F32),
            pltpu.VMEM((tm, ffn), BF16),
        ],
        compiler_params=_params("arbitrary", "arbitrary"),
        name="mix_ffn",
    )(x, mod, mix, att, w_out, gain, up, cw, cb, down, final_g)


def _tile_rows(s):
    pick = lambda top: max(t for t in (256, 512, 1024) if t <= top and s % t == 0)
    return pick(1024), pick(512), pick(1024)


def _pad_lanes(v, width=LANES):
    return jnp.pad(v, ((0, 0), (0, width - v.shape[-1])))


def kernel(x, c, positions, ada_w, ada_b, norm1_g, w_in, ssd_conv_w, ssd_conv_b, ssd_dt_bias, ssd_a_log,
           ssd_d, ssd_norm_g, pool_w, pool_scale, w_out, norm2_g, ffn_up, ffn_conv_w, ffn_conv_b, ffn_down,
           final_g):
    depth = w_in.shape[0]
    b, s, d = x.shape
    heads = ssd_dt_bias.shape[-1]
    inner = heads * SSD_HEAD_DIM
    conv_ch = ssd_conv_w.shape[-1]
    pool_width = pool_scale.shape[-1]
    ssd_proj = inner + conv_ch + heads
    att_w = (w_in.shape[-1] - ssd_proj - pool_width) // 3
    widths = (inner, conv_ch, pool_width, att_w)
    assert conv_ch == inner + 2 * SSD_GROUPS * SSD_STATE and att_w % LANES == 0
    assert s % (ATT_BLOCK * ATT_PATTERNS[-1][1]) == 0 and s % SSD_CHUNK == 0
    tm_in, tm_ffn, ssd_rows = _tile_rows(s)

    mod_all = _ada_mod(c, ada_w, ada_b).reshape(depth, b, 6, d)
    mod_all = jnp.pad(mod_all, ((0, 0), (0, 0), (0, SUBLANES - 6), (0, 0)))
    cos, sin = _rope_tables(positions)
    a0 = ssd_proj

    for i in range(depth):
        mod = mod_all[i]
        wi = w_in[i]
        a1 = a0 + pool_width
        qk_cols = [wi[:, a1 + t * att_w + p * LANES:a1 + t * att_w + (p + 1) * LANES]
                   for p in range(att_w // LANES) for t in range(2)]
        w_main = jnp.concatenate([wi[:, :inner + conv_ch], wi[:, a0:a1], *qk_cols, wi[:, a1 + 2 * att_w:]],
                                 axis=1).astype(BF16)
        w_dt = _pad_lanes(wi[:, inner + conv_ch:a0]).astype(BF16)
        z, xbc, dt, u, *views = _in_projection(x, mod, norm1_g[i][None], w_main, w_dt, cos, sin, widths, tm_in)
        pw_bd = jax.scipy.linalg.block_diag(*[pool_w[i, g] for g in range(pool_w.shape[1])]).astype(BF16)
        mix = _ssd_pool(
            z, xbc, dt, u, ssd_conv_w[i], ssd_conv_b[i][None], _pad_lanes(ssd_dt_bias[i][None]),
            _pad_lanes(ssd_a_log[i][None]), jnp.repeat(ssd_d[i], SSD_HEAD_DIM)[None], ssd_norm_g[i][None],
            pw_bd, pool_scale[i][None], heads, ssd_rows)
        att = _dilated_attention(views)
        x = _mix_ffn(x, mod, mix, att, w_out[i].astype(BF16), norm2_g[i][None], ffn_up[i].astype(BF16),
                     ffn_conv_w[i], ffn_conv_b[i][None], ffn_down[i].astype(BF16), final_g[None],
                     i == depth - 1, tm_ffn)
    return x
```

```python
import functools

import jax
import jax.numpy as jnp
from jax import lax
from jax.experimental import pallas as pl
from jax.experimental.pallas import tpu as pltpu

F32 = jnp.float32
BF16 = jnp.bfloat16

LANES = 128
SUBLANES = 8
VMEM_LIMIT = 56 * 1024 * 1024

NORM_EPS = 1e-6
ROPE_THETA = 500000.0
LOG2_E = 1.4426950408889634

SSD_HEAD_DIM = 64
SSD_GROUPS = 2
SSD_STATE = 128
SSD_CONV_K = 4
SSD_CHUNK = 256
POOL_WINDOWS = (2, 4, 8, 16)
ATT_HEAD_DIM = 64
ATT_PATTERNS = ((128, 1), (512, 4), (2048, 16))
ATT_BLOCK = 128
ATT_ORDER = tuple(sorted(ATT_PATTERNS, key=lambda wd: -wd[1]))
ATT_UNROLL_MAX = 8
ROT_DIM = ATT_HEAD_DIM // 4
FFN_CONV_K = 3

CONV_HALO = SUBLANES
POOL_HALO = 2 * SUBLANES


def _params(*semantics):
    return pltpu.CompilerParams(dimension_semantics=semantics, vmem_limit_bytes=VMEM_LIMIT)


def _resident(a):
    return pl.BlockSpec(a.shape, lambda *_: (0,) * a.ndim, pipeline_mode=pl.Buffered(1))


def _silu(v):
    return v * jax.nn.sigmoid(v)


def _dot(a, b):
    return jnp.dot(a, b, preferred_element_type=F32)


def _dot_nt(a, b):
    return lax.dot_general(a, b, (((1,), (1,)), ((), ())), preferred_element_type=F32)


def _ada_kernel(c_ref, w_ref, b_ref, o_ref):
    c_act = _silu(c_ref[...])
    o_ref[0] = jnp.dot(c_act, w_ref[0], preferred_element_type=F32,
                       precision=lax.Precision.HIGHEST) + b_ref[0]


def _ada_mod(c, ada_w, ada_b):
    depth, d, n = ada_w.shape
    b = c.shape[0]
    tn = d
    return pl.pallas_call(
        _ada_kernel,
        out_shape=jax.ShapeDtypeStruct((depth, b, n), F32),
        grid=(depth, n // tn),
        in_specs=[
            pl.BlockSpec((b, d), lambda i, j: (0, 0)),
            pl.BlockSpec((1, d, tn), lambda i, j: (i, 0, j)),
            pl.BlockSpec((1, 1, tn), lambda i, j: (i, 0, j)),
        ],
        out_specs=pl.BlockSpec((1, b, tn), lambda i, j: (i, 0, j)),
        compiler_params=_params("arbitrary", "arbitrary"),
        name="ada_mod",
    )(c, ada_w, ada_b.reshape(depth, 1, n))


def _rope_kernel(pos_ref, freq_ref, ecos_ref, esin_ref, cos_ref, sin_ref):
    ang = freq_ref[...] * pos_ref[0]
    rows = [jnp.cos(ang), jnp.sin(ang), jnp.ones_like(ang)]
    rows.append(jnp.zeros((LANES - len(rows) * ang.shape[0], ang.shape[1]), F32))
    t = jnp.concatenate(rows, axis=0).T
    cos_ref[0] = jnp.dot(t, ecos_ref[...], preferred_element_type=F32, precision=lax.Precision.HIGHEST)
    sin_ref[0] = jnp.dot(t, esin_ref[...], preferred_element_type=F32, precision=lax.Precision.HIGHEST)


def _rope_tables(positions):
    b, s = positions.shape
    half = ROT_DIM // 2
    assert half == SUBLANES
    inv_freq = ROPE_THETA ** (-jnp.arange(0, ROT_DIM, 2, dtype=F32) / ROT_DIM)
    lane = jnp.arange(LANES)[None, :] % ATT_HEAD_DIM
    row = jnp.arange(LANES)[:, None]
    rotary = lane < ROT_DIM
    ecos = jnp.where(rotary, row == lane % half, row == 2 * half).astype(F32)
    sign = jnp.where(lane < half, -1.0, 1.0)
    esin = jnp.where(rotary & (row == half + lane % half), sign, 0.0).astype(F32)
    return pl.pallas_call(
        _rope_kernel,
        out_shape=(jax.ShapeDtypeStruct((b, s, LANES), F32),) * 2,
        grid=(b,),
        in_specs=[
            pl.BlockSpec((1, 1, s), lambda i: (i, 0, 0)),
            pl.BlockSpec((half, 1), lambda i: (0, 0)),
            pl.BlockSpec((LANES, LANES), lambda i: (0, 0)),
            pl.BlockSpec((LANES, LANES), lambda i: (0, 0)),
        ],
        out_specs=(pl.BlockSpec((1, s, LANES), lambda i: (i, 0, 0)),) * 2,
        compiler_params=_params("arbitrary"),
        name="rope_tables",
    )(positions.astype(F32)[:, None, :], inv_freq[:, None], ecos, esin)


def _modulated_norm(x, gain, shift, scale):
    y = x * lax.rsqrt(jnp.mean(x * x, axis=-1, keepdims=True) + NORM_EPS)
    return (y * gain) * (1.0 + scale) + shift


def _rotate(t, cos, sin):
    half = ROT_DIM // 2
    lane = lax.broadcasted_iota(jnp.int32, t.shape, 1) % ATT_HEAD_DIM
    partner = jnp.where(lane < half, pltpu.roll(t, LANES - half, 1), pltpu.roll(t, half, 1))
    return t * cos + partner * sin


def _inproj_kernel(widths, x_ref, mod_ref, g_ref, w_ref, wdt_ref, cos_ref, sin_ref,
                   z_ref, xbc_ref, dt_ref, u_ref, *rest):
    qkv_refs, slabs = rest[:-1], rest[-1]
    inner, conv_ch, pool_w, att_w = widths
    tm = x_ref.shape[1]
    h = _modulated_norm(x_ref[0], g_ref[...], mod_ref[0, 0:1, :], mod_ref[0, 1:2, :]).astype(BF16)
    o = 0
    z_ref[0] = _silu(_dot(h, w_ref[:, o:o + inner])).astype(z_ref.dtype)
    o += inner
    xbc_ref[0] = _dot(h, w_ref[:, o:o + conv_ch]).astype(xbc_ref.dtype)
    o += conv_ch
    u_ref[0] = _dot(h, w_ref[:, o:o + pool_w]).astype(u_ref.dtype)
    o += pool_w
    dt_ref[0] = _dot(h, wdt_ref[...])
    cos, sin = cos_ref[0], sin_ref[0]
    scale = ATT_HEAD_DIM ** -0.5 * LOG2_E
    pairs = att_w // LANES

    def emit(t, which, p):
        slab = slabs.at[which * pairs + p]
        slab[...] = t
        for bi, (_, dil) in enumerate(ATT_ORDER):
            ref = qkv_refs[3 * bi + which]
            if dil == 1:
                ref[p, 0] = t.astype(BF16)
            else:
                for r in range(dil):
                    ref[p, 0, :, r * LANES:(r + 1) * LANES] = (
                        slab[pl.ds(r, tm // dil, stride=dil), :].astype(BF16))

    for p in range(pairs):
        qk = _dot(h, w_ref[:, o + 2 * p * LANES:o + 2 * (p + 1) * LANES])
        emit(_rotate(qk[:, :LANES], cos, sin) * scale, 0, p)
        emit(_rotate(qk[:, LANES:], cos, sin), 1, p)
    v = _dot(h, w_ref[:, o + 2 * att_w:o + 3 * att_w])
    for p in range(pairs):
        emit(v[:, p * LANES:(p + 1) * LANES], 2, p)


def _in_projection(x, mod, gain, w_main, w_dt, cos, sin, widths, tm):
    b, s, d = x.shape
    inner, conv_ch, pool_w, att_w = widths
    pairs = att_w // LANES
    row = lambda i, j: (i, j, 0)
    view_shapes, view_specs = [], []
    for _, dil in ATT_ORDER:
        assert tm % (dil * 2 * SUBLANES) == 0
        view_shapes += [jax.ShapeDtypeStruct((pairs, b, s // dil, dil * LANES), BF16)] * 3
        view_specs += [pl.BlockSpec((pairs, 1, tm // dil, dil * LANES), lambda i, j: (0, i, j, 0))] * 3
    return pl.pallas_call(
        functools.partial(_inproj_kernel, widths),
        out_shape=(
            jax.ShapeDtypeStruct((b, s, inner), BF16),
            jax.ShapeDtypeStruct((b, s, conv_ch), BF16),
            jax.ShapeDtypeStruct((b, s, LANES), F32),
            jax.ShapeDtypeStruct((b, s, pool_w), BF16),
            *view_shapes,
        ),
        grid=(b, s // tm),
        in_specs=[
            pl.BlockSpec((1, tm, d), row),
            pl.BlockSpec((1, SUBLANES, d), lambda i, j: (i, 0, 0)),
            _resident(gain), _resident(w_main), _resident(w_dt),
            pl.BlockSpec((1, tm, LANES), row),
            pl.BlockSpec((1, tm, LANES), row),
        ],
        out_specs=(
            pl.BlockSpec((1, tm, inner), row),
            pl.BlockSpec((1, tm, conv_ch), row),
            pl.BlockSpec((1, tm, LANES), row),
            pl.BlockSpec((1, tm, pool_w), row),
            *view_specs,
        ),
        scratch_shapes=[pltpu.VMEM((3 * pairs, tm, LANES), F32)],
        compiler_params=_params("arbitrary", "arbitrary"),
        name="in_projection",
    )(x, mod, gain, w_main, w_dt, cos, sin)


def _pair_rep(col, h0):
    lane = lax.broadcasted_iota(jnp.int32, (col.shape[0], LANES), 1)
    return jnp.where(lane < SSD_HEAD_DIM, col[:, h0:h0 + 1], col[:, h0 + 1:h0 + 2])


def _ssd_pool_kernel(dims, z_ref, xbc_ref, dt_ref, u_ref, cw_ref, cb_ref, dtb_ref, alog_ref, dskip_ref,
                     ng_ref, pw_ref, ps_ref, o_ref, xbuf, ubuf, hstate):
    inner, heads = dims
    q = SSD_CHUNK
    blk_rows = z_ref.shape[1]
    gw = inner // SSD_GROUPS
    n = SSD_STATE
    step = pl.program_id(1)

    @pl.when(step == 0)
    def _():
        xbuf[0:CONV_HALO, :] = jnp.zeros((CONV_HALO, xbuf.shape[1]), F32)
        ubuf[0:POOL_HALO, :] = jnp.zeros((POOL_HALO, ubuf.shape[1]), F32)
        hstate[...] = jnp.zeros(hstate.shape, F32)

    xbuf[CONV_HALO:CONV_HALO + blk_rows, :] = xbc_ref[0].astype(F32)
    ext = xbuf[...]
    conv = cb_ref[...] + cw_ref[SSD_CONV_K - 1:SSD_CONV_K, :] * ext
    for k in range(1, SSD_CONV_K):
        conv = conv + cw_ref[SSD_CONV_K - 1 - k:SSD_CONV_K - k, :] * pltpu.roll(ext, k, 0)
    xbc_all = _silu(conv[CONV_HALO:, :])
    xbuf[0:CONV_HALO, :] = ext[blk_rows:blk_rows + CONV_HALO, :]

    lane = lax.broadcasted_iota(jnp.int32, (blk_rows, LANES), 1)
    dt_all = jax.nn.softplus(dt_ref[0] + dtb_ref[...])
    a = -jnp.exp(alog_ref[...])
    da_all = jnp.where(lane < heads, dt_all * a, 0.0)
    rows = lax.broadcasted_iota(jnp.int32, (q, q), 0)
    cols = lax.broadcasted_iota(jnp.int32, (q, q), 1)
    causal = rows >= cols
    tri = jnp.where(causal, 1.0, 0.0).astype(F32)
    left = lax.broadcasted_iota(jnp.int32, (q, LANES), 1) < SSD_HEAD_DIM
    hpg = heads // SSD_GROUPS

    for ci in range(blk_rows // q):
        r0 = ci * q
        xbc = xbc_all[r0:r0 + q, :]
        dt = dt_all[r0:r0 + q, :]
        a_col = jnp.dot(tri, da_all[r0:r0 + q, :], preferred_element_type=F32,
                        precision=lax.Precision.HIGHEST)
        a_row = a_col.T
        y_groups = []
        for g in range(SSD_GROUPS):
            xs = xbc[:, g * gw:(g + 1) * gw]
            bm_f = xbc[:, inner + g * n:inner + (g + 1) * n]
            bm = bm_f.astype(BF16)
            bm_t = bm_f.T.astype(BF16)
            cm = xbc[:, inner + SSD_GROUPS * n + g * n:inner + SSD_GROUPS * n + (g + 1) * n].astype(BF16)
            cb = _dot_nt(cm, bm)
            y_off = _dot(cm, hstate[g].astype(BF16))
            y_pairs, xd_pairs, cd_pairs = [], [], []
            for p in range(hpg // 2):
                h0 = g * hpg + 2 * p
                xs_p = xs[:, p * LANES:(p + 1) * LANES]
                acum = _pair_rep(a_col, h0)
                xdt = xs_p * _pair_rep(dt, h0)
                y_p = y_off[:, p * LANES:(p + 1) * LANES] * jnp.exp(acum)
                for side in range(2):
                    hh = h0 + side
                    seg = a_col[:, hh:hh + 1] - a_row[hh:hh + 1, :]
                    lmat = jnp.exp(jnp.where(causal, seg, -jnp.inf))
                    keep = left if side == 0 else jnp.logical_not(left)
                    y_p = y_p + _dot((cb * lmat).astype(BF16), jnp.where(keep, xdt, 0.0).astype(BF16))
                a_last = acum[q - 1:q, :]
                xd_pairs.append((xdt * jnp.exp(a_last - acum)).astype(BF16))
                cd_pairs.append(jnp.exp(a_last))
                y_pairs.append(y_p + dskip_ref[:, h0 * SSD_HEAD_DIM:(h0 + 2) * SSD_HEAD_DIM] * xs_p)
            xd = jnp.concatenate(xd_pairs, axis=1)
            hstate[g] = hstate[g] * jnp.concatenate(cd_pairs, axis=1) + _dot(bm_t, xd)
            y = jnp.concatenate(y_pairs, axis=1) * z_ref[0, r0:r0 + q, g * gw:(g + 1) * gw].astype(F32)
            y = y * lax.rsqrt(jnp.mean(y * y, axis=-1, keepdims=True) + NORM_EPS)
            y_groups.append(y * ng_ref[:, g * gw:(g + 1) * gw])
        o_ref[0, r0:r0 + q, 0:inner] = jnp.concatenate(y_groups, axis=1).astype(o_ref.dtype)

    pool_w = ubuf.shape[1]
    pool_ch = pool_w // len(POOL_WINDOWS)
    u = u_ref[0].astype(F32)
    ubuf[POOL_HALO:POOL_HALO + blk_rows, :] = u
    ext = ubuf[...]
    plane = lax.broadcasted_iota(jnp.int32, (blk_rows, pool_w), 1)
    pos = step * blk_rows + lax.broadcasted_iota(jnp.int32, (blk_rows, pool_w), 0)
    pooled = jnp.zeros((blk_rows, pool_w), F32)
    win = jnp.zeros((blk_rows, pool_w), jnp.int32)
    acc, width = ext, 1
    for gi, w in enumerate(POOL_WINDOWS):
        while width < w:
            acc = acc + pltpu.roll(acc, width, 0)
            width *= 2
        assert width == w
        in_group = (plane >= gi * pool_ch) & (plane < (gi + 1) * pool_ch)
        pooled = jnp.where(in_group, acc[POOL_HALO:, :], pooled)
        win = jnp.where(in_group, w, win)
    cnt = jnp.minimum(pos + 1, win).astype(F32)
    diff = (pooled / cnt - u).astype(BF16)
    ubuf[0:POOL_HALO, :] = ext[blk_rows:blk_rows + POOL_HALO, :]
    o_ref[0, :, inner:inner + pool_w] = (_dot(diff, pw_ref[...]) * ps_ref[...]).astype(o_ref.dtype)


def _ssd_pool(z, xbc, dt, u, cw, cb, dtb, alog, dskip, ng, pw, ps, heads, blk_rows):
    b, s, inner = z.shape
    conv_ch = xbc.shape[-1]
    pool_w = u.shape[-1]
    assert blk_rows % SSD_CHUNK == 0 and s % blk_rows == 0
    row = lambda i, j: (i, j, 0)
    full = _resident
    out_w = inner + pool_w
    return pl.pallas_call(
        functools.partial(_ssd_pool_kernel, (inner, heads)),
        out_shape=jax.ShapeDtypeStruct((b, s, out_w), BF16),
        grid=(b, s // blk_rows),
        in_specs=[
            pl.BlockSpec((1, blk_rows, inner), row),
            pl.BlockSpec((1, blk_rows, conv_ch), row),
            pl.BlockSpec((1, blk_rows, LANES), row),
            pl.BlockSpec((1, blk_rows, pool_w), row),
            full(cw), full(cb), full(dtb), full(alog), full(dskip), full(ng), full(pw), full(ps),
        ],
        out_specs=pl.BlockSpec((1, blk_rows, out_w), row),
        scratch_shapes=[
            pltpu.VMEM((CONV_HALO + blk_rows, conv_ch), F32),
            pltpu.VMEM((POOL_HALO + blk_rows, pool_w), F32),
            pltpu.VMEM((SSD_GROUPS, SSD_STATE, inner // SSD_GROUPS), F32),
        ],
        compiler_params=_params("arbitrary", "arbitrary"),
        name="ssd_pool",
    )(z, xbc, dt, u, cw, cb, dtb, alog, dskip, ng, pw, ps)


def _attn_kernel(seq, *refs):
    nbr = len(ATT_ORDER)
    qkv_refs = refs[:3 * nbr]
    o_ref = refs[3 * nbr]
    acc_ref, l_ref, m0_ref, m1_ref = refs[3 * nbr + 1:]
    blk = ATT_BLOCK
    hd = ATT_HEAD_DIM
    lane = lax.broadcasted_iota(jnp.int32, (blk, LANES), 1)
    left = lane < hd
    qi = lax.broadcasted_iota(jnp.int32, (blk, blk), 0)
    kj = lax.broadcasted_iota(jnp.int32, (blk, blk), 1)
    mask_cur = kj <= qi
    mask_prev = kj >= qi
    neg = -jnp.inf

    mask_prev2 = jnp.concatenate([mask_prev, mask_prev], axis=0)
    mask_cur2 = jnp.concatenate([mask_cur, mask_cur], axis=0)

    def softmax_blocks(qs, kbands, vbands, pens, states):
        nb = len(qs)
        zero = jnp.zeros_like(qs[0])
        scores = []
        for u in range(nb):
            q2 = jnp.concatenate([jnp.where(left, qs[u], zero), jnp.where(left, zero, qs[u])], axis=0)
            s = _dot_nt(q2, kbands[u])
            s_p = s[:, :blk] if pens[u] is None else s[:, :blk] + pens[u]
            scores.append((jnp.where(mask_prev2, s_p, neg), jnp.where(mask_cur2, s[:, blk:], neg)))
        m_new = []
        for u in range(nb):
            s_p, s_c = scores[u]
            m_blk = jnp.max(jnp.maximum(s_p, s_c), axis=-1, keepdims=True)
            if states[u] is None:
                m_new.append(jnp.broadcast_to(m_blk, (2 * blk, LANES)))
            else:
                m_new.append(jnp.maximum(jnp.concatenate(states[u][:2], axis=0), m_blk))
        res, sums = [], []
        for u in range(nb):
            s_p, s_c = scores[u]
            p_p, p_c = jnp.exp2(s_p - m_new[u]), jnp.exp2(s_c - m_new[u])
            sums.append(jnp.sum(p_p + p_c, axis=-1, keepdims=True))
            res.append(_dot(jnp.concatenate([p_p, p_c], axis=1).astype(BF16), vbands[u]))
        outs = []
        for u in range(nb):
            acc_new = jnp.where(left, res[u][:blk], res[u][blk:])
            l_new = jnp.where(left, sums[u][:blk], sums[u][blk:])
            m0_new, m1_new = m_new[u][:blk], m_new[u][blk:]
            if states[u] is not None:
                alpha = jnp.where(left, jnp.exp2(states[u][0] - m0_new), jnp.exp2(states[u][1] - m1_new))
                acc_new = alpha * states[u][2] + acc_new
                l_new = alpha * states[u][3] + l_new
            outs.append((m0_new, m1_new, acc_new, l_new))
        return outs

    for bi, (window, dil) in enumerate(ATT_ORDER):
        assert window // dil == blk
        q_ref, k_ref, v_ref = qkv_refs[3 * bi:3 * bi + 3]
        nblk = seq // dil // blk
        first = bi == 0
        last = bi == nbr - 1
        unroll = min(ATT_UNROLL_MAX, nblk)
        assert nblk % unroll == 0 and (not last or dil == 1)
        for r in range(dil):
            col = r * LANES

            def body(it, carry, first=first, last=last, dil=dil, r=r, col=col, unroll=unroll,
                     q_ref=q_ref, k_ref=k_ref, v_ref=v_ref):
                n0 = it * unroll

                def tile(ref, n):
                    if isinstance(n, int):
                        return ref[0, 0, n * blk:(n + 1) * blk, col:col + LANES]
                    return ref[0, 0, pl.ds(pl.multiple_of(n * blk, blk), blk), col:col + LANES]

                def state_rows(n):
                    if dil == 1:
                        return pl.ds(n * blk if isinstance(n, int) else pl.multiple_of(n * blk, blk), blk)
                    return pl.ds(n * (blk * dil) + r, blk, stride=dil)

                n_prev = max(n0 - 1, 0) if isinstance(n0, int) else jnp.maximum(n0 - 1, 0)
                ks = [tile(k_ref, n_prev)] + [tile(k_ref, n0 + u) for u in range(unroll)]
                vs = [tile(v_ref, n_prev)] + [tile(v_ref, n0 + u) for u in range(unroll)]
                qs = [tile(q_ref, n0 + u) for u in range(unroll)]
                rows = [state_rows(n0 + u) for u in range(unroll)]
                states = [None if first else (m0_ref[rw, :], m1_ref[rw, :], acc_ref[rw, :], l_ref[rw, :])
                          for rw in rows]
                if isinstance(n0, int):
                    pen0 = None if n0 > 0 else neg
                else:
                    pen0 = jnp.where(n0 > 0, 0.0, neg).astype(F32)
                kbands = [jnp.concatenate([ks[u], ks[u + 1]], axis=0) for u in range(unroll)]
                vbands = [jnp.concatenate([vs[u], vs[u + 1]], axis=0) for u in range(unroll)]
                outs = softmax_blocks(qs, kbands, vbands, [pen0] + [None] * (unroll - 1), states)
                for rw, (m0_new, m1_new, acc_new, l_new) in zip(rows, outs):
                    if last:
                        o_ref[0, 0, rw, :] = (acc_new / l_new).astype(o_ref.dtype)
                    else:
                        m0_ref[rw, :] = m0_new
                        m1_ref[rw, :] = m1_new
                        acc_ref[rw, :] = acc_new
                        l_ref[rw, :] = l_new
                return carry

            if nblk == unroll:
                body(0, 0)
            else:
                lax.fori_loop(0, nblk // unroll, body, 0)


def _dilated_attention(views):
    pairs, b = views[-1].shape[:2]
    s = views[-1].shape[2] * ATT_ORDER[-1][1]
    specs = [pl.BlockSpec((1, 1) + t.shape[2:], lambda i, j: (j, i, 0, 0)) for t in views]
    return pl.pallas_call(
        functools.partial(_attn_kernel, s),
        out_shape=jax.ShapeDtypeStruct((pairs, b, s, LANES), BF16),
        grid=(b, pairs),
        in_specs=specs,
        out_specs=pl.BlockSpec((1, 1, s, LANES), lambda i, j: (j, i, 0, 0)),
        scratch_shapes=[pltpu.VMEM((s, LANES), F32)] * 4,
        compiler_params=_params("arbitrary", "arbitrary"),
        name="dilated_attention",
    )(*views)


FFN_COLS = 256
OUT_COLS = 256


def _mix_ffn_kernel(cfg, x_ref, mod_ref, mix_ref, att_ref, wout_ref, g_ref, up_ref, cw_ref, cb_ref, down_ref,
                    fg_ref, o_ref, x1, carry, act):
    ffn, final_norm = cfg
    tm, d = x_ref.shape[1:]
    fc = FFN_COLS
    halo = SUBLANES
    mw = mix_ref.shape[-1]

    for c0 in range(0, d, OUT_COLS):
        y = _dot(mix_ref[0], wout_ref[0:mw, c0:c0 + OUT_COLS])
        for p in range(att_ref.shape[0]):
            y = y + _dot(att_ref[p, 0], wout_ref[mw + p * LANES:mw + (p + 1) * LANES, c0:c0 + OUT_COLS])
        x1[:, c0:c0 + OUT_COLS] = x_ref[0, :, c0:c0 + OUT_COLS] + mod_ref[0, 2:3, c0:c0 + OUT_COLS] * y

    h = _modulated_norm(x1[...], g_ref[...], mod_ref[0, 3:4, :], mod_ref[0, 4:5, :]).astype(BF16)

    @pl.when(pl.program_id(1) == 0)
    def _():
        carry[...] = jnp.zeros(carry.shape, F32)

    top_row = lax.broadcasted_iota(jnp.int32, (halo, fc), 0)

    def conv_act(c0):
        hid = _dot(h, up_ref[:, c0:c0 + fc])
        prev = carry[:, c0:c0 + fc]
        carry[:, c0:c0 + fc] = hid[tm - halo:tm, :]
        out = cb_ref[:, c0:c0 + fc] + cw_ref[FFN_CONV_K - 1:FFN_CONV_K, c0:c0 + fc] * hid
        for k in range(1, FFN_CONV_K):
            sh = pltpu.roll(hid, k, 0)
            top = jnp.where(top_row < k, pltpu.roll(prev, k, 0), sh[0:halo, :])
            sh = jnp.concatenate([top, sh[halo:, :]], axis=0)
            out = out + cw_ref[FFN_CONV_K - 1 - k:FFN_CONV_K - k, c0:c0 + fc] * sh
        return out

    for j in range(ffn // fc):
        gate = conv_act(j * fc)
        val = conv_act(ffn + j * fc)
        act[:, j * fc:(j + 1) * fc] = (_silu(gate) * val).astype(BF16)

    for c0 in range(0, d, OUT_COLS):
        y = _dot(act[...], down_ref[:, c0:c0 + OUT_COLS])
        o_ref[0, :, c0:c0 + OUT_COLS] = x1[:, c0:c0 + OUT_COLS] + mod_ref[0, 5:6, c0:c0 + OUT_COLS] * y
    if final_norm:
        y = o_ref[0]
        o_ref[0] = y * lax.rsqrt(jnp.mean(y * y, axis=-1, keepdims=True) + NORM_EPS) * fg_ref[...]


def _mix_ffn(x, mod, mix, att, w_out, gain, up, cw, cb, down, final_g, final_norm, tm):
    b, s, d = x.shape
    ffn = down.shape[0]
    pairs = att.shape[0]
    assert ffn % FFN_COLS == 0 and d % OUT_COLS == 0
    row = lambda i, j: (i, j, 0)
    return pl.pallas_call(
        functools.partial(_mix_ffn_kernel, (ffn, final_norm)),
        out_shape=jax.ShapeDtypeStruct((b, s, d), F32),
        grid=(b, s // tm),
        in_specs=[
            pl.BlockSpec((1, tm, d), row),
            pl.BlockSpec((1, SUBLANES, d), lambda i, j: (i, 0, 0)),
            pl.BlockSpec((1, tm, mix.shape[-1]), row),
            pl.BlockSpec((pairs, 1, tm, LANES), lambda i, j: (0, i, j, 0)),
            _resident(w_out), _resident(gain), _resident(up), _resident(cw), _resident(cb), _resident(down),
            _resident(final_g),
        ],
        out_specs=pl.BlockSpec((1, tm, d), row),
        scratch_shapes=[
            pltpu.VMEM((tm, d), F32),
            pltpu.VMEM((SUBLANES, 2 * ffn), F32),
            pltpu.VMEM((tm, ffn), BF16),
        ],
        compiler_params=_params("arbitrary", "arbitrary"),
        name="mix_ffn",
    )(x, mod, mix, att, w_out, gain, up, cw, cb, down, final_g)


def _tile_rows(s):
    pick = lambda top: max(t for t in (256, 512, 1024) if t <= top and s % t == 0)
    return pick(1024), pick(512), pick(1024)


def _pad_lanes(v, width=LANES):
    return jnp.pad(v, ((0, 0), (0, width - v.shape[-1])))


def kernel(x, c, positions, ada_w, ada_b, norm1_g, w_in, ssd_conv_w, ssd_conv_b, ssd_dt_bias, ssd_a_log,
           ssd_d, ssd_norm_g, pool_w, pool_scale, w_out, norm2_g, ffn_up, ffn_conv_w, ffn_conv_b, ffn_down,
           final_g):
    depth = w_in.shape[0]
    b, s, d = x.shape
    heads = ssd_dt_bias.shape[-1]
    inner = heads * SSD_HEAD_DIM
    conv_ch = ssd_conv_w.shape[-1]
    pool_width = pool_scale.shape[-1]
    ssd_proj = inner + conv_ch + heads
    att_w = (w_in.shape[-1] - ssd_proj - pool_width) // 3
    widths = (inner, conv_ch, pool_width, att_w)
    assert conv_ch == inner + 2 * SSD_GROUPS * SSD_STATE and att_w % LANES == 0
    assert s % (ATT_BLOCK * ATT_PATTERNS[-1][1]) == 0 and s % SSD_CHUNK == 0
    tm_in, tm_ffn, ssd_rows = _tile_rows(s)

    mod_all = _ada_mod(c, ada_w, ada_b).reshape(depth, b, 6, d)
    mod_all = jnp.pad(mod_all, ((0, 0), (0, 0), (0, SUBLANES - 6), (0, 0)))
    cos, sin = _rope_tables(positions)
    a0 = ssd_proj

    for i in range(depth):
        mod = mod_all[i]
        wi = w_in[i]
        a1 = a0 + pool_width
        qk_cols = [wi[:, a1 + t * att_w + p * LANES:a1 + t * att_w + (p + 1) * LANES]
                   for p in range(att_w // LANES) for t in range(2)]
        w_main = jnp.concatenate([wi[:, :inner + conv_ch], wi[:, a0:a1], *qk_cols, wi[:, a1 + 2 * att_w:]],
                                 axis=1).astype(BF16)
        w_dt = _pad_lanes(wi[:, inner + conv_ch:a0]).astype(BF16)
        z, xbc, dt, u, *views = _in_projection(x, mod, norm1_g[i][None], w_main, w_dt, cos, sin, widths, tm_in)
        pw_bd = jax.scipy.linalg.block_diag(*[pool_w[i, g] for g in range(pool_w.shape[1])]).astype(BF16)
        mix = _ssd_pool(
            z, xbc, dt, u, ssd_conv_w[i], ssd_conv_b[i][None], _pad_lanes(ssd_dt_bias[i][None]),
            _pad_lanes(ssd_a_log[i][None]), jnp.repeat(ssd_d[i], SSD_HEAD_DIM)[None], ssd_norm_g[i][None],
            pw_bd, pool_scale[i][None], heads, ssd_rows)
        att = _dilated_attention(views)
        x = _mix_ffn(x, mod, mix, att, w_out[i].astype(BF16), norm2_g[i][None], ffn_up[i].astype(BF16),
                     ffn_conv_w[i], ffn_conv_b[i][None], ffn_down[i].astype(BF16), final_g[None],
                     i == depth - 1, tm_ffn)
    return x
```
